```python
import jax
import jax.numpy as jnp
from jax import lax
import numpy as np

D_MODEL = 2048
BATCH = 2
SEQ = 4096
DEPTH = 4
DEC_BATCH = 8
DEC_SEQ = 4
PAST_LEN = 16384
PAGE_SIZE = 128

NORM_EPS = 1e-6
BRANCH_WIDTH = 1024
LRU_WIDTH = BRANCH_WIDTH
LRU_BLOCKS = 16
LRU_BLOCK_DIM = LRU_WIDTH // LRU_BLOCKS
LRU_CONV = 4
LRU_C = 8.0
NSA_HEADS = 16
NSA_KV_HEADS = 4
HEAD_DIM = 64
NSA_WIDTH = NSA_HEADS * HEAD_DIM
CMP_STRIDE = 16
CMP_BLOCK = 2 * CMP_STRIDE
SLC_BLOCK = 64
N_SELECT = 16
WINDOW = 512
Q_BLOCK = 128
ROPE_THETA = 10000.0
FORCE_BONUS = 1000.0
NEG_INF = -1e30
RWKV_HEADS = 16
RWKV_HEAD_DIM = 64
RWKV_WIDTH = RWKV_HEADS * RWKV_HEAD_DIM
DECAY_LORA = 64
ICL_LORA = 64
GATE_LORA = 160
RWKV_GN_EPS = 64e-5
D_FF = 3 * D_MODEL
FFN_CONV = 3

N_KV_COLS = 6 * NSA_KV_HEADS * HEAD_DIM
N_NSA_GATES = 3 * NSA_HEADS
RWKV_COLS = 3 * RWKV_WIDTH + DECAY_LORA + ICL_LORA + GATE_LORA
IN_SPLITS = (LRU_WIDTH, LRU_WIDTH + NSA_WIDTH, LRU_WIDTH + NSA_WIDTH + N_KV_COLS, LRU_WIDTH + NSA_WIDTH + N_KV_COLS + N_NSA_GATES, LRU_WIDTH + NSA_WIDTH + N_KV_COLS + N_NSA_GATES + RWKV_COLS)
D_IN = IN_SPLITS[-1] + 3 * D_MODEL
RWKV_SPLITS = (RWKV_WIDTH, 2 * RWKV_WIDTH, 3 * RWKV_WIDTH, 3 * RWKV_WIDTH + DECAY_LORA, 3 * RWKV_WIDTH + DECAY_LORA + ICL_LORA)

kernel_name = 'hybrid_lru_nsa_rwkv7_step'


def _rmsnorm(x, g):
    xf = x.astype(jnp.float32)
    y = xf * lax.rsqrt(jnp.mean(xf * xf, axis=-1, keepdims=True) + NORM_EPS)
    return (y * g.astype(jnp.float32)).astype(x.dtype)


def _causal_conv(x, buf, w, b):
    k, t = w.shape[0], x.shape[1]
    xp = jnp.concatenate([buf.astype(x.dtype), x], axis=1)
    y = b + xp[:, 0:t] * w[0]
    for j in range(1, k):
        y = y + xp[:, j:j + t] * w[j]
    return y, xp[:, xp.shape[1] - (k - 1):]


def _rope(x, pos):
    half = x.shape[-1] // 2
    inv = ROPE_THETA ** (-jnp.arange(half, dtype=jnp.float32) / half)
    ang = pos.astype(jnp.float32)[:, None] * inv[None, :]
    cos = jnp.cos(ang)[None, :, None, :].astype(x.dtype)
    sin = jnp.sin(ang)[None, :, None, :].astype(x.dtype)
    x1, x2 = x[..., :half], x[..., half:]
    return jnp.concatenate([x1 * cos - x2 * sin, x2 * cos + x1 * sin], axis=-1)


def _masked_softmax(s, mask):
    p = jax.nn.softmax(jnp.where(mask, s, NEG_INF), axis=-1)
    return jnp.where(mask, p, 0.0)


def _lin_combine(left, right):
    a1, b1 = left
    a2, b2 = right
    return a1 * a2, a2 * b1 + b2


def _pad_rows(z, mult):
    pad = -z.shape[1] % mult
    return jnp.pad(z, ((0, 0), (0, pad)) + ((0, 0),) * (z.ndim - 2))


def _rg_lru(xa, h0, conv_buf, conv_w, conv_b, gate_w, gate_b, lam):
    f32 = jnp.float32
    bsz, t, _ = xa.shape
    xc, new_buf = _causal_conv(xa, conv_buf, conv_w, conv_b)
    xb = xc.reshape(bsz, t, LRU_BLOCKS, LRU_BLOCK_DIM)
    gates = jnp.einsum('btnd,gnde->gbtne', xb, gate_w).reshape(2, bsz, t, LRU_WIDTH)
    gates = gates.astype(f32) + gate_b.astype(f32)[:, None, None, :]
    r, i = jax.nn.sigmoid(gates[0]), jax.nn.sigmoid(gates[1])
    log_a = -LRU_C * r * jax.nn.softplus(-lam.astype(f32))
    a = jnp.exp(log_a)
    b = jnp.sqrt(-jnp.expm1(2.0 * log_a)) * (i * xc.astype(f32))
    b = b.at[:, 0].add(a[:, 0] * h0.astype(f32))
    _, h = lax.associative_scan(_lin_combine, (a, b), axis=1)
    return h.astype(xa.dtype), h[:, -1].astype(xa.dtype), new_buf


def _compress(z, w, b):
    bsz, length, g, d = z.shape
    ch = z.reshape(bsz, length // CMP_STRIDE, CMP_STRIDE, g, d)
    head = jnp.einsum('bcjgd,jde->bcge', ch, w[:CMP_STRIDE])
    tail = jnp.einsum('bcjgd,jde->bcge', ch, w[CMP_STRIDE:])
    return head[:, :-1] + tail[:, 1:] + b


def _nsa_core(q, q_rot, qpos, kc, vc, fetch, kw, vw, kwpos, gates):
    f32 = jnp.float32
    bsz, t = q.shape[0], q.shape[1]
    hpg = NSA_HEADS // NSA_KV_HEADS
    scale = HEAD_DIM ** -0.5
    qg = q.reshape(bsz, t, NSA_KV_HEADS, hpg, HEAD_DIM)
    qrg = q_rot.reshape(bsz, t, NSA_KV_HEADS, hpg, HEAD_DIM)
    n_cmp = kc.shape[1]
    cmp_end = jnp.arange(n_cmp) * CMP_STRIDE + (CMP_BLOCK - 1)
    m_c = (cmp_end[None, :] <= qpos[:, None])[None, :, None, None, :]
    p_c = _masked_softmax(jnp.einsum('btghd,bngd->btghn', qg, kc).astype(f32) * scale, m_c)
    o_c = jnp.einsum('btghn,bngd->btghd', p_c.astype(vc.dtype), vc)
    per = SLC_BLOCK // CMP_STRIDE
    n_slc = (n_cmp + 1) // per
    imp = jnp.pad(p_c.sum(axis=3), ((0, 0), (0, 0), (0, 0), (0, 1))).reshape(bsz, t, NSA_KV_HEADS, n_slc, per)
    imp = imp.sum(-1) + jnp.pad(imp[..., :-1, per - 1], ((0, 0), (0, 0), (0, 0), (1, 0)))
    blk = jnp.arange(n_slc)[None, :]
    qblk = (qpos // SLC_BLOCK)[:, None]
    valid = blk * SLC_BLOCK <= qpos[:, None]
    forced = (blk == 0) | (blk == qblk) | (blk == qblk - 1)
    score = jnp.where(valid[None, :, None, :], imp + FORCE_BONUS * forced[None, :, None, :], NEG_INF)
    n_top = min(N_SELECT, n_slc)
    _, idx = lax.top_k(score, n_top)
    tok = (idx[..., None] * SLC_BLOCK + jnp.arange(SLC_BLOCK)).reshape(bsz, t, NSA_KV_HEADS, n_top * SLC_BLOCK)
    m_s = (tok <= qpos[None, :, None, None])[:, :, :, None, :]
    k_sel, v_sel = fetch(tok)
    p_s = _masked_softmax(jnp.einsum('btghd,btgkd->btghk', qrg, k_sel).astype(f32) * scale, m_s)
    o_s = jnp.einsum('btghk,btgkd->btghd', p_s.astype(v_sel.dtype), v_sel)
    dpos = qpos[:, None] - kwpos[None, :]
    m_w = ((dpos >= 0) & (dpos < WINDOW) & (kwpos[None, :] >= 0))[None, :, None, None, :]
    p_w = _masked_softmax(jnp.einsum('btghd,bkgd->btghk', qrg, kw).astype(f32) * scale, m_w)
    o_w = jnp.einsum('btghk,bkgd->btghd', p_w.astype(vw.dtype), vw)
    gt = jax.nn.sigmoid(gates.astype(f32)).astype(q.dtype).reshape(bsz, t, NSA_KV_HEADS, hpg, 3, 1)
    o = gt[..., 0, :] * o_c + gt[..., 1, :] * o_s + gt[..., 2, :] * o_w
    return o.reshape(bsz, t, NSA_WIDTH)


def _nsa_prompt(q, kv, gates, phi, phi_b):
    bsz, s = q.shape[0], q.shape[1]
    pos = jnp.arange(s)
    k_cmp, v_cmp, k_slc, v_slc, k_win, v_win = [kv[:, :, i] for i in range(6)]
    q_rot, k_slc, k_win = _rope(q, pos), _rope(k_slc, pos), _rope(k_win, pos)
    kc = _compress(_pad_rows(k_cmp, SLC_BLOCK), phi[0], phi_b[0])
    vc = _compress(_pad_rows(v_cmp, SLC_BLOCK), phi[1], phi_b[1])
    pad_w = ((0, 0), (WINDOW, 0), (0, 0), (0, 0))
    kw_pad, vw_pad = jnp.pad(k_win, pad_w), jnp.pad(v_win, pad_w)
    bi = jnp.arange(bsz)[:, None, None, None]
    gi = jnp.arange(NSA_KV_HEADS)[None, None, :, None]

    def fetch(tok):
        tok = jnp.minimum(tok, s - 1)
        return k_slc[bi, tok, gi], v_slc[bi, tok, gi]

    n_qb = s // Q_BLOCK

    def blocks(z):
        return jnp.moveaxis(z.reshape((bsz, n_qb, Q_BLOCK) + z.shape[2:]), 1, 0)

    def one_block(args):
        i, qb, qrb, gb = args
        start = i * Q_BLOCK
        kw = lax.dynamic_slice_in_dim(kw_pad, start, WINDOW + Q_BLOCK, axis=1)
        vw = lax.dynamic_slice_in_dim(vw_pad, start, WINDOW + Q_BLOCK, axis=1)
        kwpos = start - WINDOW + jnp.arange(WINDOW + Q_BLOCK)
        return _nsa_core(qb, qrb, start + jnp.arange(Q_BLOCK), kc, vc, fetch, kw, vw, kwpos, gb)

    o = lax.map(one_block, (jnp.arange(n_qb), blocks(q), blocks(q_rot), blocks(gates)))
    o = jnp.moveaxis(o, 0, 1).reshape(bsz, s, NSA_WIDTH)
    n_win = min(WINDOW, s)
    rows = jnp.stack([k_cmp, v_cmp, k_slc, v_slc], axis=2)
    win_rows = jnp.stack([k_win, v_win], axis=2)[:, s - n_win:]
    return o, rows, win_rows


def _nsa_sample(q, kv, gates, phi, phi_b, pool, page_table, win_buf):
    bsz, t = q.shape[0], q.shape[1]
    pos = PAST_LEN + jnp.arange(t)
    k_cmp, v_cmp, k_slc, v_slc, k_win, v_win = [kv[:, :, i] for i in range(6)]
    q_rot, k_slc, k_win = _rope(q, pos), _rope(k_slc, pos), _rope(k_win, pos)
    n_pages = PAST_LEN // PAGE_SIZE
    past = pool[page_table, :, :2].reshape(bsz, n_pages * PAGE_SIZE, 2, NSA_KV_HEADS, HEAD_DIM)
    kc = _compress(_pad_rows(jnp.concatenate([past[:, :, 0], k_cmp], axis=1), SLC_BLOCK), phi[0], phi_b[0])
    vc = _compress(_pad_rows(jnp.concatenate([past[:, :, 1], v_cmp], axis=1), SLC_BLOCK), phi[1], phi_b[1])
    bi = jnp.arange(bsz)[:, None, None, None]
    gi = jnp.arange(NSA_KV_HEADS)[None, None, :, None]

    def fetch(tok):
        tp = jnp.clip(tok, 0, PAST_LEN - 1)
        phys = page_table[bi, tp // PAGE_SIZE]
        off = tp % PAGE_SIZE
        tn = jnp.clip(tok - PAST_LEN, 0, t - 1)
        is_new = (tok >= PAST_LEN)[..., None]
        k_g = jnp.where(is_new, k_slc[bi, tn, gi], pool[phys, off, 2, gi])
        v_g = jnp.where(is_new, v_slc[bi, tn, gi], pool[phys, off, 3, gi])
        return k_g, v_g

    kw = jnp.concatenate([win_buf[:, :, 0].astype(q.dtype), k_win], axis=1)
    vw = jnp.concatenate([win_buf[:, :, 1].astype(q.dtype), v_win], axis=1)
    n_win = win_buf.shape[1]
    kwpos = PAST_LEN - n_win + jnp.arange(n_win + t)
    o = _nsa_core(q, q_rot, pos, kc, vc, fetch, kw, vw, kwpos, gates)
    rows = jnp.stack([k_cmp, v_cmp, k_slc, v_slc], axis=2)
    win_rows = jnp.stack([k_win, v_win], axis=2)
    return o, rows, win_rows


def _rwkv7(c, wkv0, shift0, mu, w0, w2, a0, a2, g2, k_k, k_a, r_k, ln_g, ln_b):
    f32 = jnp.float32
    bsz, t, _ = c.shape
    prev = jnp.concatenate([shift0.astype(c.dtype), c[:, :-1]], axis=1)
    cm = c + mu * (prev - c)
    r, k, v, wl, al, gl = jnp.split(cm, RWKV_SPLITS, axis=-1)
    log_w = -jax.nn.softplus(-(w0 + jnp.tanh(wl) @ w2).astype(f32)) - 0.5
    decay = jnp.exp(-jnp.exp(log_w))
    a = jax.nn.sigmoid((a0 + al @ a2).astype(f32))
    g = jax.nn.sigmoid(gl) @ g2

    def heads(z):
        return z.astype(f32).reshape(bsz, t, RWKV_HEADS, RWKV_HEAD_DIM)

    kk = heads(k * k_k)
    kk = kk * lax.rsqrt(jnp.sum(kk * kk, axis=-1, keepdims=True) + 1e-12)
    k = k.astype(f32) * (1.0 + (a - 1.0) * k_a.astype(f32))
    rh, kh, vh, wh, ah = heads(r), heads(k), heads(v), heads(decay), heads(a)

    def step(state, inp):
        r_t, w_t, k_t, v_t, kk_t, a_t = inp
        sa = jnp.einsum('bhvk,bhk->bhv', state, kk_t)
        state = state * w_t[:, :, None, :] - sa[..., None] * (kk_t * a_t)[:, :, None, :] + v_t[..., None] * k_t[:, :, None, :]
        return state, jnp.einsum('bhvk,bhk->bhv', state, r_t)

    def tm(z):
        return jnp.moveaxis(z, 1, 0)

    s_fin, y = lax.scan(step, wkv0.astype(f32), (tm(rh), tm(wh), tm(kh), tm(vh), tm(kk), tm(ah)))
    y = jnp.moveaxis(y, 0, 1)
    mean = jnp.mean(y, axis=-1, keepdims=True)
    var = jnp.mean(jnp.square(y - mean), axis=-1, keepdims=True)
    y = ((y - mean) * lax.rsqrt(var + RWKV_GN_EPS)).reshape(bsz, t, RWKV_WIDTH) * ln_g.astype(f32) + ln_b.astype(f32)
    bonus = (jnp.sum(rh * kh * r_k.astype(f32), axis=-1, keepdims=True) * vh).reshape(bsz, t, RWKV_WIDTH)
    out = ((y + bonus) * g.astype(f32)).astype(c.dtype)
    return out, s_fin.astype(c.dtype), c[:, t - 1:]


def _layer(x, l, P, st, nsa_fn):
    lru_h0, lru_conv0, wkv0, shift0, ffn_conv0 = st
    bsz, t, _ = x.shape
    norms = P['norms'][l]
    h = _rmsnorm(x, norms[0])
    proj = h @ P['w_in'][l]
    xa, q, kv, nsa_g, rw, mg = jnp.split(proj, IN_SPLITS, axis=-1)
    o_a, lru_h, lru_conv = _rg_lru(xa, lru_h0, lru_conv0, P['lru_conv_w'][l], P['lru_conv_b'][l], P['lru_gate_w'][l], P['lru_gate_b'][l], P['lru_lambda'][l])
    q = q.reshape(bsz, t, NSA_HEADS, HEAD_DIM)
    kv = kv.reshape(bsz, t, 6, NSA_KV_HEADS, HEAD_DIM)
    o_b, nsa_rows, win_rows = nsa_fn(l, q, kv, nsa_g)
    o_c, wkv, shift = _rwkv7(rw, wkv0, shift0, P['rwkv_mu'][l], P['rwkv_w0'][l], P['rwkv_w2'][l], P['rwkv_a0'][l], P['rwkv_a2'][l], P['rwkv_g2'][l], P['rwkv_k_k'][l], P['rwkv_k_a'][l], P['rwkv_r_k'][l], P['rwkv_ln_g'][l], P['rwkv_ln_b'][l])
    g_a, g_b, g_c = jnp.split(jax.nn.sigmoid(mg), 3, axis=-1)
    wb = P['w_branch'][l]
    merged = g_a * (o_a @ wb[0]) + g_b * (o_b @ wb[1]) + g_c * (o_c @ wb[2])
    x = x + _rmsnorm(merged @ P['w_out'][l], norms[1])
    u, ffn_conv = _causal_conv(_rmsnorm(x, norms[2]) @ P['w_up'][l], ffn_conv0, P['ffn_conv_w'][l], P['ffn_conv_b'][l])
    u_gate, u_val = jnp.split(u, 2, axis=-1)
    x = x + _rmsnorm((jax.nn.gelu(u_gate) * u_val) @ P['w_down'][l], norms[3])
    return x, (nsa_rows, win_rows, lru_h, lru_conv, wkv, shift, ffn_conv)


def setup_inputs(seed: int = 0) -> dict:
    key = jax.random.key(seed)
    keys = iter(jax.random.split(key, 48))
    f32 = jnp.float32

    def nrm(shape, scale):
        return scale * jax.random.normal(next(keys), shape, f32)

    def unif(shape, lo, hi):
        return jax.random.uniform(next(keys), shape, f32, lo, hi)

    n_pages = PAST_LEN // PAGE_SIZE
    n_used = DEC_BATCH * n_pages
    n_phys = n_used + n_used // 4
    n_win = min(WINDOW, PAST_LEN)
    page_table = jax.random.permutation(next(keys), n_phys)[:n_used].reshape(DEC_BATCH, n_pages).astype(jnp.int32)
    a_base = unif((DEPTH, LRU_WIDTH), 0.9, 0.999) ** (1.0 / LRU_C)
    return {
        'x_prompt': nrm((BATCH, SEQ, D_MODEL), 1.0),
        'x_sample': nrm((DEC_BATCH, DEC_SEQ, D_MODEL), 1.0),
        'cache_nsa': nrm((DEPTH, n_phys, PAGE_SIZE, 4, NSA_KV_HEADS, HEAD_DIM), 1.0),
        'cache_win': nrm((DEPTH, DEC_BATCH, n_win, 2, NSA_KV_HEADS, HEAD_DIM), 1.0),
        'state_lru_h': nrm((DEPTH, DEC_BATCH, LRU_WIDTH), 0.5),
        'state_lru_conv': nrm((DEPTH, DEC_BATCH, LRU_CONV - 1, LRU_WIDTH), 1.0),
        'state_rwkv_wkv': nrm((DEPTH, DEC_BATCH, RWKV_HEADS, RWKV_HEAD_DIM, RWKV_HEAD_DIM), 0.5),
        'state_rwkv_shift': nrm((DEPTH, DEC_BATCH, 1, RWKV_COLS), 1.0),
        'state_ffn_conv': nrm((DEPTH, DEC_BATCH, FFN_CONV - 1, 2 * D_FF), 1.0),
        'page_table': page_table,
        'norms': 1.0 + nrm((DEPTH, 4, D_MODEL), 0.02),
        'w_in': nrm((DEPTH, D_MODEL, D_IN), D_MODEL ** -0.5),
        'lru_conv_w': nrm((DEPTH, LRU_CONV, LRU_WIDTH), LRU_CONV ** -0.5),
        'lru_conv_b': nrm((DEPTH, LRU_WIDTH), 0.02),
        'lru_gate_w': nrm((DEPTH, 2, LRU_BLOCKS, LRU_BLOCK_DIM, LRU_BLOCK_DIM), LRU_BLOCK_DIM ** -0.5),
        'lru_gate_b': nrm((DEPTH, 2, LRU_WIDTH), 0.02),
        'lru_lambda': jnp.log(a_base) - jnp.log1p(-a_base),
        'nsa_phi': nrm((DEPTH, 2, CMP_BLOCK, HEAD_DIM, HEAD_DIM), (CMP_BLOCK * HEAD_DIM) ** -0.5),
        'nsa_phi_b': nrm((DEPTH, 2, HEAD_DIM), 0.02),
        'rwkv_mu': unif((DEPTH, RWKV_COLS), 0.0, 1.0),
        'rwkv_w0': unif((DEPTH, RWKV_WIDTH), -6.0, -1.0),
        'rwkv_w2': nrm((DEPTH, DECAY_LORA, RWKV_WIDTH), 0.1 * DECAY_LORA ** -0.5),
        'rwkv_a0': nrm((DEPTH, RWKV_WIDTH), 0.1),
        'rwkv_a2': nrm((DEPTH, ICL_LORA, RWKV_WIDTH), 0.1 * ICL_LORA ** -0.5),
        'rwkv_g2': nrm((DEPTH, GATE_LORA, RWKV_WIDTH), GATE_LORA ** -0.5),
        'rwkv_k_k': 0.85 + nrm((DEPTH, RWKV_WIDTH), 0.02),
        'rwkv_k_a': 1.0 + nrm((DEPTH, RWKV_WIDTH), 0.02),
        'rwkv_r_k': nrm((DEPTH, RWKV_HEADS, RWKV_HEAD_DIM), 0.1),
        'rwkv_ln_g': 1.0 + nrm((DEPTH, RWKV_WIDTH), 0.02),
        'rwkv_ln_b': nrm((DEPTH, RWKV_WIDTH), 0.02),
        'w_branch': nrm((DEPTH, 3, BRANCH_WIDTH, D_MODEL), BRANCH_WIDTH ** -0.5),
        'w_out': nrm((DEPTH, D_MODEL, D_MODEL), D_MODEL ** -0.5),
        'w_up': nrm((DEPTH, D_MODEL, 2 * D_FF), D_MODEL ** -0.5),
        'ffn_conv_w': nrm((DEPTH, FFN_CONV, 2 * D_FF), FFN_CONV ** -0.5),
        'ffn_conv_b': nrm((DEPTH, 2 * D_FF), 0.02),
        'w_down': nrm((DEPTH, D_FF, D_MODEL), D_FF ** -0.5),
    }


def reference(x_prompt, x_sample, cache_nsa, cache_win, state_lru_h, state_lru_conv, state_rwkv_wkv, state_rwkv_shift, state_ffn_conv, page_table, norms, w_in, lru_conv_w, lru_conv_b, lru_gate_w, lru_gate_b, lru_lambda, nsa_phi, nsa_phi_b, rwkv_mu, rwkv_w0, rwkv_w2, rwkv_a0, rwkv_a2, rwkv_g2, rwkv_k_k, rwkv_k_a, rwkv_r_k, rwkv_ln_g, rwkv_ln_b, w_branch, w_out, w_up, ffn_conv_w, ffn_conv_b, w_down):
    P = {'norms': norms, 'w_in': w_in, 'lru_conv_w': lru_conv_w, 'lru_conv_b': lru_conv_b, 'lru_gate_w': lru_gate_w, 'lru_gate_b': lru_gate_b, 'lru_lambda': lru_lambda, 'rwkv_mu': rwkv_mu, 'rwkv_w0': rwkv_w0, 'rwkv_w2': rwkv_w2, 'rwkv_a0': rwkv_a0, 'rwkv_a2': rwkv_a2, 'rwkv_g2': rwkv_g2, 'rwkv_k_k': rwkv_k_k, 'rwkv_k_a': rwkv_k_a, 'rwkv_r_k': rwkv_r_k, 'rwkv_ln_g': rwkv_ln_g, 'rwkv_ln_b': rwkv_ln_b, 'w_branch': w_branch, 'w_out': w_out, 'w_up': w_up, 'ffn_conv_w': ffn_conv_w, 'ffn_conv_b': ffn_conv_b, 'w_down': w_down}

    def nsa_prompt_fn(l, q, kv, g):
        return _nsa_prompt(q, kv, g, nsa_phi[l], nsa_phi_b[l])

    def nsa_sample_fn(l, q, kv, g):
        return _nsa_sample(q, kv, g, nsa_phi[l], nsa_phi_b[l], cache_nsa[l], page_table, cache_win[l])

    bsz, dt = x_prompt.shape[0], x_prompt.dtype
    zero_state = (jnp.zeros((bsz, LRU_WIDTH), dt), jnp.zeros((bsz, LRU_CONV - 1, LRU_WIDTH), dt), jnp.zeros((bsz, RWKV_HEADS, RWKV_HEAD_DIM, RWKV_HEAD_DIM), dt), jnp.zeros((bsz, 1, RWKV_COLS), dt), jnp.zeros((bsz, FFN_CONV - 1, 2 * D_FF), dt))
    y_p, y_s = x_prompt, x_sample
    new_p, new_s = [], []
    for l in range(DEPTH):
        y_p, st_p = _layer(y_p, l, P, zero_state, nsa_prompt_fn)
        y_s, st_s = _layer(y_s, l, P, (state_lru_h[l], state_lru_conv[l], state_rwkv_wkv[l], state_rwkv_shift[l], state_ffn_conv[l]), nsa_sample_fn)
        new_p.append(st_p)
        new_s.append(st_s)

    def stacked(rows, i):
        return jnp.stack([r[i] for r in rows])

    return (y_p, y_s, stacked(new_p, 0), stacked(new_p, 1), stacked(new_p, 2), stacked(new_p, 3), stacked(new_p, 4), stacked(new_p, 5), stacked(new_p, 6), stacked(new_s, 0), stacked(new_s, 1), stacked(new_s, 2), stacked(new_s, 3), stacked(new_s, 4), stacked(new_s, 5), stacked(new_s, 6))
```

```python
import functools

import jax
import jax.numpy as jnp
from jax import lax
from jax.experimental import pallas as pl
from jax.experimental.pallas import tpu as pltpu

D_MODEL = 2048
DEPTH = 4
PAST_LEN = 16384
PAGE_SIZE = 128
NORM_EPS = 1e-6
BRANCH_WIDTH = 1024
LRU_WIDTH = BRANCH_WIDTH
LRU_BLOCKS = 16
LRU_BLOCK_DIM = LRU_WIDTH // LRU_BLOCKS
LRU_CONV = 4
LRU_C = 8.0
NSA_HEADS = 16
NSA_KV_HEADS = 4
HEAD_DIM = 64
NSA_WIDTH = NSA_HEADS * HEAD_DIM
CMP_STRIDE = 16
CMP_BLOCK = 2 * CMP_STRIDE
SLC_BLOCK = 64
N_SELECT = 16
WINDOW = 512
Q_BLOCK = 128
ROPE_THETA = 10000.0
FORCE_BONUS = 1000.0
NEG_INF = -1e30
RWKV_HEADS = 16
RWKV_HEAD_DIM = 64
RWKV_WIDTH = RWKV_HEADS * RWKV_HEAD_DIM
DECAY_LORA = 64
ICL_LORA = 64
GATE_LORA = 160
RWKV_GN_EPS = 64e-5
D_FF = 3 * D_MODEL
FFN_CONV = 3
N_KV_COLS = 6 * NSA_KV_HEADS * HEAD_DIM
N_NSA_GATES = 3 * NSA_HEADS
RWKV_COLS = 3 * RWKV_WIDTH + DECAY_LORA + ICL_LORA + GATE_LORA
IN_SPLITS = (LRU_WIDTH, LRU_WIDTH + NSA_WIDTH, LRU_WIDTH + NSA_WIDTH + N_KV_COLS,
             LRU_WIDTH + NSA_WIDTH + N_KV_COLS + N_NSA_GATES,
             LRU_WIDTH + NSA_WIDTH + N_KV_COLS + N_NSA_GATES + RWKV_COLS)
D_IN = IN_SPLITS[-1] + 3 * D_MODEL
RWKV_SPLITS = (RWKV_WIDTH, 2 * RWKV_WIDTH, 3 * RWKV_WIDTH, 3 * RWKV_WIDTH + DECAY_LORA,
               3 * RWKV_WIDTH + DECAY_LORA + ICL_LORA)


def _mm_kernel(x_ref, w_ref, o_ref):
    @pl.when(pl.program_id(2) == 0)
    def _():
        o_ref[...] = jnp.zeros_like(o_ref)

    o_ref[...] += jnp.dot(x_ref[...].astype(jnp.bfloat16), w_ref[...].astype(jnp.bfloat16),
                          preferred_element_type=jnp.float32)


def _mm(x, w, tm=512, tn=512, tk=1024):
    m, k = x.shape
    n = w.shape[1]
    tm = min(tm, m)
    tk = min(tk, k)
    assert m % tm == 0 and k % tk == 0
    return pl.pallas_call(
        _mm_kernel,
        grid=(m // tm, pl.cdiv(n, tn), k // tk),
        in_specs=[pl.BlockSpec((tm, tk), lambda i, j, kk: (i, kk)),
                  pl.BlockSpec((tk, tn), lambda i, j, kk: (kk, j))],
        out_specs=pl.BlockSpec((tm, tn), lambda i, j, kk: (i, j)),
        out_shape=jax.ShapeDtypeStruct((m, n), jnp.float32),
        compiler_params=pltpu.CompilerParams(
            dimension_semantics=("parallel", "parallel", "arbitrary")),
        name="mm",
    )(x, w)


def _mm3(x, w):
    b, t, k = x.shape
    return _mm(x.reshape(b * t, k), w).reshape(b, t, w.shape[1])


def _rmsnorm(x, g):
    xf = x.astype(jnp.float32)
    y = xf * lax.rsqrt(jnp.mean(xf * xf, axis=-1, keepdims=True) + NORM_EPS)
    return (y * g.astype(jnp.float32)).astype(x.dtype)


def _causal_conv(x, buf, w, b):
    k, t = w.shape[0], x.shape[1]
    xp = jnp.concatenate([buf.astype(x.dtype), x], axis=1)
    y = b + xp[:, 0:t] * w[0]
    for j in range(1, k):
        y = y + xp[:, j:j + t] * w[j]
    return y, xp[:, xp.shape[1] - (k - 1):]


def _rope(x, pos):
    half = x.shape[-1] // 2
    inv = ROPE_THETA ** (-jnp.arange(half, dtype=jnp.float32) / half)
    ang = pos.astype(jnp.float32)[:, None] * inv[None, :]
    cos = jnp.cos(ang)[None, :, None, :].astype(x.dtype)
    sin = jnp.sin(ang)[None, :, None, :].astype(x.dtype)
    x1, x2 = x[..., :half], x[..., half:]
    return jnp.concatenate([x1 * cos - x2 * sin, x2 * cos + x1 * sin], axis=-1)


def _masked_softmax(s, mask):
    p = jax.nn.softmax(jnp.where(mask, s, NEG_INF), axis=-1)
    return jnp.where(mask, p, 0.0)


def _lin_combine(left, right):
    a1, b1 = left
    a2, b2 = right
    return a1 * a2, a2 * b1 + b2


def _pad_rows(z, mult):
    pad = -z.shape[1] % mult
    return jnp.pad(z, ((0, 0), (0, pad)) + ((0, 0),) * (z.ndim - 2))


def _rg_lru(xa, h0, conv_buf, conv_w, conv_b, gate_w, gate_b, lam):
    f32 = jnp.float32
    bsz, t, _ = xa.shape
    xc, new_buf = _causal_conv(xa, conv_buf, conv_w, conv_b)
    xb = xc.reshape(bsz, t, LRU_BLOCKS, LRU_BLOCK_DIM)
    gates = jnp.einsum('btnd,gnde->gbtne', xb, gate_w).reshape(2, bsz, t, LRU_WIDTH)
    gates = gates.astype(f32) + gate_b.astype(f32)[:, None, None, :]
    r, i = jax.nn.sigmoid(gates[0]), jax.nn.sigmoid(gates[1])
    log_a = -LRU_C * r * jax.nn.softplus(-lam.astype(f32))
    a = jnp.exp(log_a)
    b = jnp.sqrt(-jnp.expm1(2.0 * log_a)) * (i * xc.astype(f32))
    b = b.at[:, 0].add(a[:, 0] * h0.astype(f32))
    _, h = lax.associative_scan(_lin_combine, (a, b), axis=1)
    return h.astype(xa.dtype), h[:, -1].astype(xa.dtype), new_buf


def _compress(z, w, b):
    bsz, length, g, d = z.shape
    ch = z.reshape(bsz, length // CMP_STRIDE, CMP_STRIDE, g, d)
    head = jnp.einsum('bcjgd,jde->bcge', ch, w[:CMP_STRIDE])
    tail = jnp.einsum('bcjgd,jde->bcge', ch, w[CMP_STRIDE:])
    return head[:, :-1] + tail[:, 1:] + b


def _nsa_core(q, q_rot, qpos, kc, vc, fetch, kw, vw, kwpos, gates):
    f32 = jnp.float32
    bsz, t = q.shape[0], q.shape[1]
    hpg = NSA_HEADS // NSA_KV_HEADS
    scale = HEAD_DIM ** -0.5
    qg = q.reshape(bsz, t, NSA_KV_HEADS, hpg, HEAD_DIM)
    qrg = q_rot.reshape(bsz, t, NSA_KV_HEADS, hpg, HEAD_DIM)
    n_cmp = kc.shape[1]
    cmp_end = jnp.arange(n_cmp) * CMP_STRIDE + (CMP_BLOCK - 1)
    m_c = (cmp_end[None, :] <= qpos[:, None])[None, :, None, None, :]
    p_c = _masked_softmax(jnp.einsum('btghd,bngd->btghn', qg, kc).astype(f32) * scale, m_c)
    o_c = jnp.einsum('btghn,bngd->btghd', p_c.astype(vc.dtype), vc)
    per = SLC_BLOCK // CMP_STRIDE
    n_slc = (n_cmp + 1) // per
    imp = jnp.pad(p_c.sum(axis=3), ((0, 0), (0, 0), (0, 0), (0, 1))).reshape(bsz, t, NSA_KV_HEADS, n_slc, per)
    imp = imp.sum(-1) + jnp.pad(imp[..., :-1, per - 1], ((0, 0), (0, 0), (0, 0), (1, 0)))
    blk = jnp.arange(n_slc)[None, :]
    qblk = (qpos // SLC_BLOCK)[:, None]
    valid = blk * SLC_BLOCK <= qpos[:, None]
    forced = (blk == 0) | (blk == qblk) | (blk == qblk - 1)
    score = jnp.where(valid[None, :, None, :], imp + FORCE_BONUS * forced[None, :, None, :], NEG_INF)
    n_top = min(N_SELECT, n_slc)
    _, idx = lax.top_k(score, n_top)
    tok = (idx[..., None] * SLC_BLOCK + jnp.arange(SLC_BLOCK)).reshape(bsz, t, NSA_KV_HEADS, n_top * SLC_BLOCK)
    m_s = (tok <= qpos[None, :, None, None])[:, :, :, None, :]
    k_sel, v_sel = fetch(tok)
    p_s = _masked_softmax(jnp.einsum('btghd,btgkd->btghk', qrg, k_sel).astype(f32) * scale, m_s)
    o_s = jnp.einsum('btghk,btgkd->btghd', p_s.astype(v_sel.dtype), v_sel)
    dpos = qpos[:, None] - kwpos[None, :]
    m_w = ((dpos >= 0) & (dpos < WINDOW) & (kwpos[None, :] >= 0))[None, :, None, None, :]
    p_w = _masked_softmax(jnp.einsum('btghd,bkgd->btghk', qrg, kw).astype(f32) * scale, m_w)
    o_w = jnp.einsum('btghk,bkgd->btghd', p_w.astype(vw.dtype), vw)
    gt = jax.nn.sigmoid(gates.astype(f32)).astype(q.dtype).reshape(bsz, t, NSA_KV_HEADS, hpg, 3, 1)
    o = gt[..., 0, :] * o_c + gt[..., 1, :] * o_s + gt[..., 2, :] * o_w
    return o.reshape(bsz, t, NSA_WIDTH)


def _nsa_prompt(q, kv, gates, phi, phi_b):
    bsz, s = q.shape[0], q.shape[1]
    pos = jnp.arange(s)
    k_cmp, v_cmp, k_slc, v_slc, k_win, v_win = [kv[:, :, i] for i in range(6)]
    q_rot, k_slc, k_win = _rope(q, pos), _rope(k_slc, pos), _rope(k_win, pos)
    kc = _compress(_pad_rows(k_cmp, SLC_BLOCK), phi[0], phi_b[0])
    vc = _compress(_pad_rows(v_cmp, SLC_BLOCK), phi[1], phi_b[1])
    pad_w = ((0, 0), (WINDOW, 0), (0, 0), (0, 0))
    kw_pad, vw_pad = jnp.pad(k_win, pad_w), jnp.pad(v_win, pad_w)
    bi = jnp.arange(bsz)[:, None, None, None]
    gi = jnp.arange(NSA_KV_HEADS)[None, None, :, None]

    def fetch(tok):
        tok = jnp.minimum(tok, s - 1)
        return k_slc[bi, tok, gi], v_slc[bi, tok, gi]

    n_qb = s // Q_BLOCK

    def blocks(z):
        return jnp.moveaxis(z.reshape((bsz, n_qb, Q_BLOCK) + z.shape[2:]), 1, 0)

    def one_block(args):
        i, qb, qrb, gb = args
        start = i * Q_BLOCK
        kw = lax.dynamic_slice_in_dim(kw_pad, start, WINDOW + Q_BLOCK, axis=1)
        vw = lax.dynamic_slice_in_dim(vw_pad, start, WINDOW + Q_BLOCK, axis=1)
        kwpos = start - WINDOW + jnp.arange(WINDOW + Q_BLOCK)
        return _nsa_core(qb, qrb, start + jnp.arange(Q_BLOCK), kc, vc, fetch, kw, vw, kwpos, gb)

    o = lax.map(one_block, (jnp.arange(n_qb), blocks(q), blocks(q_rot), blocks(gates)))
    o = jnp.moveaxis(o, 0, 1).reshape(bsz, s, NSA_WIDTH)
    n_win = min(WINDOW, s)
    rows = jnp.stack([k_cmp, v_cmp, k_slc, v_slc], axis=2)
    win_rows = jnp.stack([k_win, v_win], axis=2)[:, s - n_win:]
    return o, rows, win_rows


def _nsa_sample(q, kv, gates, phi, phi_b, pool, page_table, win_buf):
    bsz, t = q.shape[0], q.shape[1]
    pos = PAST_LEN + jnp.arange(t)
    k_cmp, v_cmp, k_slc, v_slc, k_win, v_win = [kv[:, :, i] for i in range(6)]
    q_rot, k_slc, k_win = _rope(q, pos), _rope(k_slc, pos), _rope(k_win, pos)
    n_pages = PAST_LEN // PAGE_SIZE
    past = pool[page_table, :, :2].reshape(bsz, n_pages * PAGE_SIZE, 2, NSA_KV_HEADS, HEAD_DIM)
    kc = _compress(_pad_rows(jnp.concatenate([past[:, :, 0], k_cmp], axis=1), SLC_BLOCK), phi[0], phi_b[0])
    vc = _compress(_pad_rows(jnp.concatenate([past[:, :, 1], v_cmp], axis=1), SLC_BLOCK), phi[1], phi_b[1])
    bi = jnp.arange(bsz)[:, None, None, None]
    gi = jnp.arange(NSA_KV_HEADS)[None, None, :, None]

    def fetch(tok):
        tp = jnp.clip(tok, 0, PAST_LEN - 1)
        phys = page_table[bi, tp // PAGE_SIZE]
        off = tp % PAGE_SIZE
        tn = jnp.clip(tok - PAST_LEN, 0, t - 1)
        is_new = (tok >= PAST_LEN)[..., None]
        k_g = jnp.where(is_new, k_slc[bi, tn, gi], pool[phys, off, 2, gi])
        v_g = jnp.where(is_new, v_slc[bi, tn, gi], pool[phys, off, 3, gi])
        return k_g, v_g

    kw = jnp.concatenate([win_buf[:, :, 0].astype(q.dtype), k_win], axis=1)
    vw = jnp.concatenate([win_buf[:, :, 1].astype(q.dtype), v_win], axis=1)
    n_win = win_buf.shape[1]
    kwpos = PAST_LEN - n_win + jnp.arange(n_win + t)
    o = _nsa_core(q, q_rot, pos, kc, vc, fetch, kw, vw, kwpos, gates)
    rows = jnp.stack([k_cmp, v_cmp, k_slc, v_slc], axis=2)
    win_rows = jnp.stack([k_win, v_win], axis=2)
    return o, rows, win_rows


def _rwkv7(c, wkv0, shift0, mu, w0, w2, a0, a2, g2, k_k, k_a, r_k, ln_g, ln_b):
    f32 = jnp.float32
    bsz, t, _ = c.shape
    prev = jnp.concatenate([shift0.astype(c.dtype), c[:, :-1]], axis=1)
    cm = c + mu * (prev - c)
    r, k, v, wl, al, gl = jnp.split(cm, RWKV_SPLITS, axis=-1)
    log_w = -jax.nn.softplus(-(w0 + jnp.tanh(wl) @ w2).astype(f32)) - 0.5
    decay = jnp.exp(-jnp.exp(log_w))
    a = jax.nn.sigmoid((a0 + al @ a2).astype(f32))
    g = jax.nn.sigmoid(gl) @ g2

    def heads(z):
        return z.astype(f32).reshape(bsz, t, RWKV_HEADS, RWKV_HEAD_DIM)

    kk = heads(k * k_k)
    kk = kk * lax.rsqrt(jnp.sum(kk * kk, axis=-1, keepdims=True) + 1e-12)
    k = k.astype(f32) * (1.0 + (a - 1.0) * k_a.astype(f32))
    rh, kh, vh, wh, ah = heads(r), heads(k), heads(v), heads(decay), heads(a)

    def step(state, inp):
        r_t, w_t, k_t, v_t, kk_t, a_t = inp
        sa = jnp.einsum('bhvk,bhk->bhv', state, kk_t)
        state = state * w_t[:, :, None, :] - sa[..., None] * (kk_t * a_t)[:, :, None, :] + v_t[..., None] * k_t[:, :, None, :]
        return state, jnp.einsum('bhvk,bhk->bhv', state, r_t)

    def tm(z):
        return jnp.moveaxis(z, 1, 0)

    s_fin, y = lax.scan(step, wkv0.astype(f32), (tm(rh), tm(wh), tm(kh), tm(vh), tm(kk), tm(ah)))
    y = jnp.moveaxis(y, 0, 1)
    mean = jnp.mean(y, axis=-1, keepdims=True)
    var = jnp.mean(jnp.square(y - mean), axis=-1, keepdims=True)
    y = ((y - mean) * lax.rsqrt(var + RWKV_GN_EPS)).reshape(bsz, t, RWKV_WIDTH) * ln_g.astype(f32) + ln_b.astype(f32)
    bonus = (jnp.sum(rh * kh * r_k.astype(f32), axis=-1, keepdims=True) * vh).reshape(bsz, t, RWKV_WIDTH)
    out = ((y + bonus) * g.astype(f32)).astype(c.dtype)
    return out, s_fin.astype(c.dtype), c[:, t - 1:]


def _layer(x, l, P, st, nsa_fn):
    lru_h0, lru_conv0, wkv0, shift0, ffn_conv0 = st
    bsz, t, _ = x.shape
    norms = P['norms'][l]
    h = _rmsnorm(x, norms[0])
    proj = _mm3(h, P['w_in'][l])
    xa, q, kv, nsa_g, rw, mg = jnp.split(proj, IN_SPLITS, axis=-1)
    o_a, lru_h, lru_conv = _rg_lru(xa, lru_h0, lru_conv0, P['lru_conv_w'][l], P['lru_conv_b'][l], P['lru_gate_w'][l], P['lru_gate_b'][l], P['lru_lambda'][l])
    q = q.reshape(bsz, t, NSA_HEADS, HEAD_DIM)
    kv = kv.reshape(bsz, t, 6, NSA_KV_HEADS, HEAD_DIM)
    o_b, nsa_rows, win_rows = nsa_fn(l, q, kv, nsa_g)
    o_c, wkv, shift = _rwkv7(rw, wkv0, shift0, P['rwkv_mu'][l], P['rwkv_w0'][l], P['rwkv_w2'][l], P['rwkv_a0'][l], P['rwkv_a2'][l], P['rwkv_g2'][l], P['rwkv_k_k'][l], P['rwkv_k_a'][l], P['rwkv_r_k'][l], P['rwkv_ln_g'][l], P['rwkv_ln_b'][l])
    g_a, g_b, g_c = jnp.split(jax.nn.sigmoid(mg), 3, axis=-1)
    wb = P['w_branch'][l]
    merged = g_a * _mm3(o_a, wb[0]) + g_b * _mm3(o_b, wb[1]) + g_c * _mm3(o_c, wb[2])
    x = x + _rmsnorm(_mm3(merged, P['w_out'][l]), norms[1])
    u, ffn_conv = _causal_conv(_mm3(_rmsnorm(x, norms[2]), P['w_up'][l]), ffn_conv0, P['ffn_conv_w'][l], P['ffn_conv_b'][l])
    u_gate, u_val = jnp.split(u, 2, axis=-1)
    x = x + _rmsnorm(_mm3(jax.nn.gelu(u_gate) * u_val, P['w_down'][l]), norms[3])
    return x, (nsa_rows, win_rows, lru_h, lru_conv, wkv, shift, ffn_conv)


def kernel(x_prompt, x_sample, cache_nsa, cache_win, state_lru_h, state_lru_conv, state_rwkv_wkv, state_rwkv_shift, state_ffn_conv, page_table, norms, w_in, lru_conv_w, lru_conv_b, lru_gate_w, lru_gate_b, lru_lambda, nsa_phi, nsa_phi_b, rwkv_mu, rwkv_w0, rwkv_w2, rwkv_a0, rwkv_a2, rwkv_g2, rwkv_k_k, rwkv_k_a, rwkv_r_k, rwkv_ln_g, rwkv_ln_b, w_branch, w_out, w_up, ffn_conv_w, ffn_conv_b, w_down):
    P = {'norms': norms, 'w_in': w_in, 'lru_conv_w': lru_conv_w, 'lru_conv_b': lru_conv_b, 'lru_gate_w': lru_gate_w, 'lru_gate_b': lru_gate_b, 'lru_lambda': lru_lambda, 'rwkv_mu': rwkv_mu, 'rwkv_w0': rwkv_w0, 'rwkv_w2': rwkv_w2, 'rwkv_a0': rwkv_a0, 'rwkv_a2': rwkv_a2, 'rwkv_g2': rwkv_g2, 'rwkv_k_k': rwkv_k_k, 'rwkv_k_a': rwkv_k_a, 'rwkv_r_k': rwkv_r_k, 'rwkv_ln_g': rwkv_ln_g, 'rwkv_ln_b': rwkv_ln_b, 'w_branch': w_branch, 'w_out': w_out, 'w_up': w_up, 'ffn_conv_w': ffn_conv_w, 'ffn_conv_b': ffn_conv_b, 'w_down': w_down}

    def nsa_prompt_fn(l, q, kv, g):
        return _nsa_prompt(q, kv, g, nsa_phi[l], nsa_phi_b[l])

    def nsa_sample_fn(l, q, kv, g):
        return _nsa_sample(q, kv, g, nsa_phi[l], nsa_phi_b[l], cache_nsa[l], page_table, cache_win[l])

    bsz, dt = x_prompt.shape[0], x_prompt.dtype
    zero_state = (jnp.zeros((bsz, LRU_WIDTH), dt), jnp.zeros((bsz, LRU_CONV - 1, LRU_WIDTH), dt), jnp.zeros((bsz, RWKV_HEADS, RWKV_HEAD_DIM, RWKV_HEAD_DIM), dt), jnp.zeros((bsz, 1, RWKV_COLS), dt), jnp.zeros((bsz, FFN_CONV - 1, 2 * D_FF), dt))
    y_p, y_s = x_prompt, x_sample
    new_p, new_s = [], []
    for l in range(DEPTH):
        y_p, st_p = _layer(y_p, l, P, zero_state, nsa_prompt_fn)
        y_s, st_s = _layer(y_s, l, P, (state_lru_h[l], state_lru_conv[l], state_rwkv_wkv[l], state_rwkv_shift[l], state_ffn_conv[l]), nsa_sample_fn)
        new_p.append(st_p)
        new_s.append(st_s)

    def stacked(rows, i):
        return jnp.stack([r[i] for r in rows])

    return (y_p, y_s) + tuple(stacked(new_p, i) for i in range(7)) + tuple(stacked(new_s, i) for i in range(7))
```

```python
import functools

import jax
import jax.numpy as jnp
from jax import lax
from jax.experimental import pallas as pl
from jax.experimental.pallas import tpu as pltpu

D_MODEL = 2048
DEPTH = 4
PAST_LEN = 16384
PAGE_SIZE = 128
NORM_EPS = 1e-6
BRANCH_WIDTH = 1024
LRU_WIDTH = BRANCH_WIDTH
LRU_BLOCKS = 16
LRU_BLOCK_DIM = LRU_WIDTH // LRU_BLOCKS
LRU_CONV = 4
LRU_C = 8.0
NSA_HEADS = 16
NSA_KV_HEADS = 4
HEAD_DIM = 64
NSA_WIDTH = NSA_HEADS * HEAD_DIM
CMP_STRIDE = 16
CMP_BLOCK = 2 * CMP_STRIDE
SLC_BLOCK = 64
N_SELECT = 16
WINDOW = 512
Q_BLOCK = 128
ROPE_THETA = 10000.0
FORCE_BONUS = 1000.0
NEG_INF = -1e30
RWKV_HEADS = 16
RWKV_HEAD_DIM = 64
RWKV_WIDTH = RWKV_HEADS * RWKV_HEAD_DIM
DECAY_LORA = 64
ICL_LORA = 64
GATE_LORA = 160
RWKV_GN_EPS = 64e-5
D_FF = 3 * D_MODEL
FFN_CONV = 3
N_KV_COLS = 6 * NSA_KV_HEADS * HEAD_DIM
N_NSA_GATES = 3 * NSA_HEADS
RWKV_COLS = 3 * RWKV_WIDTH + DECAY_LORA + ICL_LORA + GATE_LORA
IN_SPLITS = (LRU_WIDTH, LRU_WIDTH + NSA_WIDTH, LRU_WIDTH + NSA_WIDTH + N_KV_COLS,
             LRU_WIDTH + NSA_WIDTH + N_KV_COLS + N_NSA_GATES,
             LRU_WIDTH + NSA_WIDTH + N_KV_COLS + N_NSA_GATES + RWKV_COLS)
D_IN = IN_SPLITS[-1] + 3 * D_MODEL
RWKV_SPLITS = (RWKV_WIDTH, 2 * RWKV_WIDTH, 3 * RWKV_WIDTH, 3 * RWKV_WIDTH + DECAY_LORA,
               3 * RWKV_WIDTH + DECAY_LORA + ICL_LORA)


def _mm_kernel(x_ref, w_ref, o_ref):
    @pl.when(pl.program_id(2) == 0)
    def _():
        o_ref[...] = jnp.zeros_like(o_ref)

    o_ref[...] += jnp.dot(x_ref[...].astype(jnp.bfloat16), w_ref[...].astype(jnp.bfloat16),
                          preferred_element_type=jnp.float32)


def _mm(x, w, tm=512, tn=512, tk=1024):
    m, k = x.shape
    n = w.shape[1]
    tm = min(tm, m)
    tk = min(tk, k)
    assert m % tm == 0 and k % tk == 0
    return pl.pallas_call(
        _mm_kernel,
        grid=(m // tm, pl.cdiv(n, tn), k // tk),
        in_specs=[pl.BlockSpec((tm, tk), lambda i, j, kk: (i, kk)),
                  pl.BlockSpec((tk, tn), lambda i, j, kk: (kk, j))],
        out_specs=pl.BlockSpec((tm, tn), lambda i, j, kk: (i, j)),
        out_shape=jax.ShapeDtypeStruct((m, n), jnp.float32),
        compiler_params=pltpu.CompilerParams(
            dimension_semantics=("parallel", "parallel", "arbitrary")),
        name="mm",
    )(x, w)


def _mm3(x, w):
    b, t, k = x.shape
    return _mm(x.reshape(b * t, k), w).reshape(b, t, w.shape[1])


def _rmsnorm(x, g):
    xf = x.astype(jnp.float32)
    y = xf * lax.rsqrt(jnp.mean(xf * xf, axis=-1, keepdims=True) + NORM_EPS)
    return (y * g.astype(jnp.float32)).astype(x.dtype)


def _causal_conv(x, buf, w, b):
    k, t = w.shape[0], x.shape[1]
    xp = jnp.concatenate([buf.astype(x.dtype), x], axis=1)
    y = b + xp[:, 0:t] * w[0]
    for j in range(1, k):
        y = y + xp[:, j:j + t] * w[j]
    return y, xp[:, xp.shape[1] - (k - 1):]


def _rope(x, pos):
    half = x.shape[-1] // 2
    inv = ROPE_THETA ** (-jnp.arange(half, dtype=jnp.float32) / half)
    ang = pos.astype(jnp.float32)[:, None] * inv[None, :]
    cos = jnp.cos(ang)[None, :, None, :].astype(x.dtype)
    sin = jnp.sin(ang)[None, :, None, :].astype(x.dtype)
    x1, x2 = x[..., :half], x[..., half:]
    return jnp.concatenate([x1 * cos - x2 * sin, x2 * cos + x1 * sin], axis=-1)


def _masked_softmax(s, mask):
    p = jax.nn.softmax(jnp.where(mask, s, NEG_INF), axis=-1)
    return jnp.where(mask, p, 0.0)


def _lin_combine(left, right):
    a1, b1 = left
    a2, b2 = right
    return a1 * a2, a2 * b1 + b2


def _pad_rows(z, mult):
    pad = -z.shape[1] % mult
    return jnp.pad(z, ((0, 0), (0, pad)) + ((0, 0),) * (z.ndim - 2))


def _rg_lru(xa, h0, conv_buf, conv_w, conv_b, gate_w, gate_b, lam):
    f32 = jnp.float32
    bsz, t, _ = xa.shape
    xc, new_buf = _causal_conv(xa, conv_buf, conv_w, conv_b)
    xb = xc.reshape(bsz, t, LRU_BLOCKS, LRU_BLOCK_DIM)
    gates = jnp.einsum('btnd,gnde->gbtne', xb, gate_w).reshape(2, bsz, t, LRU_WIDTH)
    gates = gates.astype(f32) + gate_b.astype(f32)[:, None, None, :]
    r, i = jax.nn.sigmoid(gates[0]), jax.nn.sigmoid(gates[1])
    log_a = -LRU_C * r * jax.nn.softplus(-lam.astype(f32))
    a = jnp.exp(log_a)
    b = jnp.sqrt(-jnp.expm1(2.0 * log_a)) * (i * xc.astype(f32))
    b = b.at[:, 0].add(a[:, 0] * h0.astype(f32))
    _, h = lax.associative_scan(_lin_combine, (a, b), axis=1)
    return h.astype(xa.dtype), h[:, -1].astype(xa.dtype), new_buf


def _compress(z, w, b):
    bsz, length, g, d = z.shape
    ch = z.reshape(bsz, length // CMP_STRIDE, CMP_STRIDE, g, d)
    head = jnp.einsum('bcjgd,jde->bcge', ch, w[:CMP_STRIDE])
    tail = jnp.einsum('bcjgd,jde->bcge', ch, w[CMP_STRIDE:])
    return head[:, :-1] + tail[:, 1:] + b


def _nsa_core(q, q_rot, qpos, kc, vc, fetch, kw, vw, kwpos, gates):
    f32 = jnp.float32
    bsz, t = q.shape[0], q.shape[1]
    hpg = NSA_HEADS // NSA_KV_HEADS
    scale = HEAD_DIM ** -0.5
    qg = q.reshape(bsz, t, NSA_KV_HEADS, hpg, HEAD_DIM)
    qrg = q_rot.reshape(bsz, t, NSA_KV_HEADS, hpg, HEAD_DIM)
    n_cmp = kc.shape[1]
    cmp_end = jnp.arange(n_cmp) * CMP_STRIDE + (CMP_BLOCK - 1)
    m_c = (cmp_end[None, :] <= qpos[:, None])[None, :, None, None, :]
    p_c = _masked_softmax(jnp.einsum('btghd,bngd->btghn', qg, kc).astype(f32) * scale, m_c)
    o_c = jnp.einsum('btghn,bngd->btghd', p_c.astype(vc.dtype), vc)
    per = SLC_BLOCK // CMP_STRIDE
    n_slc = (n_cmp + 1) // per
    imp = jnp.pad(p_c.sum(axis=3), ((0, 0), (0, 0), (0, 0), (0, 1))).reshape(bsz, t, NSA_KV_HEADS, n_slc, per)
    imp = imp.sum(-1) + jnp.pad(imp[..., :-1, per - 1], ((0, 0), (0, 0), (0, 0), (1, 0)))
    blk = jnp.arange(n_slc)[None, :]
    qblk = (qpos // SLC_BLOCK)[:, None]
    valid = blk * SLC_BLOCK <= qpos[:, None]
    forced = (blk == 0) | (blk == qblk) | (blk == qblk - 1)
    score = jnp.where(valid[None, :, None, :], imp + FORCE_BONUS * forced[None, :, None, :], NEG_INF)
    n_top = min(N_SELECT, n_slc)
    _, idx = lax.top_k(score, n_top)
    tok = (idx[..., None] * SLC_BLOCK + jnp.arange(SLC_BLOCK)).reshape(bsz, t, NSA_KV_HEADS, n_top * SLC_BLOCK)
    m_s = (tok <= qpos[None, :, None, None])[:, :, :, None, :]
    k_sel, v_sel = fetch(tok)
    p_s = _masked_softmax(jnp.einsum('btghd,btgkd->btghk', qrg, k_sel).astype(f32) * scale, m_s)
    o_s = jnp.einsum('btghk,btgkd->btghd', p_s.astype(v_sel.dtype), v_sel)
    dpos = qpos[:, None] - kwpos[None, :]
    m_w = ((dpos >= 0) & (dpos < WINDOW) & (kwpos[None, :] >= 0))[None, :, None, None, :]
    p_w = _masked_softmax(jnp.einsum('btghd,bkgd->btghk', qrg, kw).astype(f32) * scale, m_w)
    o_w = jnp.einsum('btghk,bkgd->btghd', p_w.astype(vw.dtype), vw)
    gt = jax.nn.sigmoid(gates.astype(f32)).astype(q.dtype).reshape(bsz, t, NSA_KV_HEADS, hpg, 3, 1)
    o = gt[..., 0, :] * o_c + gt[..., 1, :] * o_s + gt[..., 2, :] * o_w
    return o.reshape(bsz, t, NSA_WIDTH)


HPG = NSA_HEADS // NSA_KV_HEADS
CMP_PER_SLC = SLC_BLOCK // CMP_STRIDE
SEL_CHUNK = 512
WIN_KEYS = WINDOW + Q_BLOCK
DROPPED = -3e38


def _softmax_rows(s, ok):
    sm = jnp.where(ok[None], s, NEG_INF)
    e = jnp.exp(sm - jnp.max(sm, axis=-1, keepdims=True))
    p = e / jnp.sum(e, axis=-1, keepdims=True)
    return jnp.where(ok[None], p, 0.0)


def _nsa_prompt_kernel(q_ref, qr_ref, g_ref, kct_ref, vc_ref, kst_ref, vs_ref, kwt_ref, vw_ref,
                       impt_ref, exp_ref, o_ref, *, n_cmp):
    f32, bf = jnp.float32, jnp.bfloat16
    i = pl.program_id(2)
    start = i * Q_BLOCK
    rows = HPG * Q_BLOCK
    n_cpad = kct_ref.shape[-1]
    n_slc = impt_ref.shape[0]
    scale = HEAD_DIM ** -0.5
    q = (q_ref[0, 0, 0] * scale).astype(bf)
    qr = (qr_ref[0, 0, 0] * scale).astype(bf)
    t_col = start + lax.broadcasted_iota(jnp.int32, (Q_BLOCK, 1), 0)

    s = jnp.dot(q, kct_ref[0, 0], preferred_element_type=f32).reshape(HPG, Q_BLOCK, n_cpad)
    n_io = lax.broadcasted_iota(jnp.int32, (Q_BLOCK, n_cpad), 1)
    ok_c = jnp.where(n_io < n_cmp, n_io * CMP_STRIDE + (CMP_BLOCK - 1), 2 ** 30) <= t_col
    p = _softmax_rows(s, ok_c)
    o_c = jnp.dot(p.reshape(rows, n_cpad).astype(bf), vc_ref[0, 0], preferred_element_type=f32)

    psum = p[0] + p[1] + p[2] + p[3]
    hi = psum.astype(bf)
    r1 = psum - hi.astype(f32)
    mid = r1.astype(bf)
    lo = (r1 - mid.astype(f32)).astype(bf)
    nt = (((1,), (1,)), ((), ()))
    imp_t = (lax.dot_general(impt_ref[...], hi, nt, preferred_element_type=f32)
             + lax.dot_general(impt_ref[...], mid, nt, preferred_element_type=f32)
             + lax.dot_general(impt_ref[...], lo, nt, preferred_element_type=f32))
    j_io = lax.broadcasted_iota(jnp.int32, (n_slc, Q_BLOCK), 0)
    t_row = start + lax.broadcasted_iota(jnp.int32, (n_slc, Q_BLOCK), 1)
    qblk = t_row // SLC_BLOCK
    forced = jnp.where(j_io == 0, 1.0, 0.0) + jnp.where(j_io == qblk, 1.0, 0.0) + jnp.where(j_io == qblk - 1, 1.0, 0.0)
    forced = jnp.minimum(forced, 1.0)
    score = jnp.where(j_io * SLC_BLOCK <= t_row, imp_t + FORCE_BONUS * forced, NEG_INF)

    def pick(_, carry):
        sc, sel = carry
        best = jnp.max(sc, axis=0, keepdims=True)
        first = jnp.min(jnp.where(sc == best, j_io, n_slc), axis=0, keepdims=True)
        hit = j_io == first
        return jnp.where(hit, DROPPED, sc), jnp.where(hit, 1.0, sel)

    _, sel_t = lax.fori_loop(0, min(N_SELECT, n_slc), pick, (score, jnp.zeros((n_slc, Q_BLOCK), f32)))
    sel = sel_t.T.astype(bf)

    def chunk(c, carry):
        m, l, acc = carry
        off = pl.multiple_of(c * SEL_CHUNK, SEL_CHUNK)
        kt = kst_ref[0, 0, :, pl.ds(off, SEL_CHUNK)]
        v = vs_ref[0, 0, pl.ds(off, SEL_CHUNK), :]
        sc = jnp.dot(qr, kt, preferred_element_type=f32).reshape(HPG, Q_BLOCK, SEL_CHUNK)
        chosen = jnp.dot(sel, exp_ref[:, pl.ds(off, SEL_CHUNK)], preferred_element_type=f32)
        kpos = off + lax.broadcasted_iota(jnp.int32, (Q_BLOCK, SEL_CHUNK), 1)
        ok = jnp.where(kpos <= t_col, chosen, 0.0) > 0.5
        sc = jnp.where(ok[None], sc, NEG_INF)
        m_new = jnp.maximum(m, jnp.max(sc, axis=-1, keepdims=True))
        alpha = jnp.exp(m - m_new)
        pe = jnp.exp(sc - m_new)
        l = alpha * l + jnp.sum(pe, axis=-1, keepdims=True)
        pv = jnp.dot(pe.reshape(rows, SEL_CHUNK).astype(bf), v, preferred_element_type=f32)
        return m_new, l, alpha * acc + pv.reshape(HPG, Q_BLOCK, HEAD_DIM)

    n_chunks = (start + Q_BLOCK + SEL_CHUNK - 1) // SEL_CHUNK
    init = (jnp.full((HPG, Q_BLOCK, 1), NEG_INF, f32), jnp.zeros((HPG, Q_BLOCK, 1), f32),
            jnp.zeros((HPG, Q_BLOCK, HEAD_DIM), f32))
    _, l_s, acc_s = lax.fori_loop(0, n_chunks, chunk, init)
    o_s = (acc_s / l_s).reshape(rows, HEAD_DIM)

    n_keys = kwt_ref.shape[-1]
    wk = min(WIN_KEYS, n_keys)
    k0 = pl.multiple_of(jnp.maximum(start + Q_BLOCK - wk, 0), Q_BLOCK)
    kt = kwt_ref[0, 0, :, pl.ds(k0, wk)]
    v = vw_ref[0, 0, pl.ds(k0, wk), :]
    sw = jnp.dot(qr, kt, preferred_element_type=f32).reshape(HPG, Q_BLOCK, wk)
    dpos = t_col - (k0 + lax.broadcasted_iota(jnp.int32, (Q_BLOCK, wk), 1))
    ok_w = jnp.where(dpos >= 0, dpos, WINDOW) < WINDOW
    pw = _softmax_rows(sw, ok_w)
    o_w = jnp.dot(pw.reshape(rows, wk).astype(bf), v, preferred_element_type=f32)

    gt = jax.nn.sigmoid(g_ref[0, 0, 0])
    o_ref[0, 0, 0] = gt[:, 0:1] * o_c + gt[:, 1:2] * o_s + gt[:, 2:3] * o_w


def _nsa_prompt_attention(q, q_rot, gates, kc, vc, k_slc, v_slc, k_win, v_win):
    bf = jnp.bfloat16
    bsz, s = q.shape[0], q.shape[1]
    g, d = NSA_KV_HEADS, HEAD_DIM
    n_qb = s // Q_BLOCK
    n_cmp = kc.shape[1]
    n_cpad = -(-(n_cmp + 1) // 128) * 128
    n_slc = (n_cmp + 1) // CMP_PER_SLC
    rows = HPG * Q_BLOCK

    def by_block(z, last):
        z = z.reshape(bsz, n_qb, Q_BLOCK, g, HPG, last)
        return z.transpose(0, 3, 1, 4, 2, 5).reshape(bsz, g, n_qb, rows, last)

    def keys_t(z):
        return z.transpose(0, 2, 3, 1).astype(bf)

    def vals(z):
        return z.transpose(0, 2, 1, 3).astype(bf)

    pad_c = ((0, 0), (0, n_cpad - n_cmp), (0, 0), (0, 0))
    n_io = jnp.arange(n_cpad)[None, :]
    j_io = jnp.arange(n_slc)[:, None]
    imp_t = ((n_io >= CMP_PER_SLC * j_io - 1) & (n_io < CMP_PER_SLC * (j_io + 1))).astype(bf)
    expand = (jnp.arange(s)[None, :] // SLC_BLOCK == j_io).astype(bf)

    qspec = pl.BlockSpec((1, 1, 1, rows, d), lambda b, gg, i: (b, gg, i, 0, 0))

    def whole(shape):
        return pl.BlockSpec((1, 1) + shape, lambda b, gg, i: (b, gg, 0, 0))

    out = pl.pallas_call(
        functools.partial(_nsa_prompt_kernel, n_cmp=n_cmp),
        grid=(bsz, g, n_qb),
        in_specs=[qspec, qspec,
                  pl.BlockSpec((1, 1, 1, rows, 3), lambda b, gg, i: (b, gg, i, 0, 0)),
                  whole((d, n_cpad)), whole((n_cpad, d)),
                  whole((d, s)), whole((s, d)), whole((d, s)), whole((s, d)),
                  pl.BlockSpec((n_slc, n_cpad), lambda b, gg, i: (0, 0)),
                  pl.BlockSpec((n_slc, s), lambda b, gg, i: (0, 0))],
        out_specs=qspec,
        out_shape=jax.ShapeDtypeStruct((bsz, g, n_qb, rows, d), jnp.float32),
        compiler_params=pltpu.CompilerParams(
            dimension_semantics=("parallel", "parallel", "arbitrary"),
            vmem_limit_bytes=48 * 1024 * 1024),
        name="nsa_prompt",
    )(by_block(q.reshape(bsz, s, -1), d), by_block(q_rot.reshape(bsz, s, -1), d), by_block(gates, 3),
      keys_t(jnp.pad(kc, pad_c)), vals(jnp.pad(vc, pad_c)),
      keys_t(k_slc), vals(v_slc), keys_t(k_win), vals(v_win), imp_t, expand)
    out = out.reshape(bsz, g, n_qb, HPG, Q_BLOCK, d).transpose(0, 2, 4, 1, 3, 5)
    return out.reshape(bsz, s, NSA_WIDTH)


def _nsa_prompt(q, kv, gates, phi, phi_b):
    bsz, s = q.shape[0], q.shape[1]
    pos = jnp.arange(s)
    k_cmp, v_cmp, k_slc, v_slc, k_win, v_win = [kv[:, :, i] for i in range(6)]
    q_rot, k_slc, k_win = _rope(q, pos), _rope(k_slc, pos), _rope(k_win, pos)
    kc = _compress(_pad_rows(k_cmp, SLC_BLOCK), phi[0], phi_b[0])
    vc = _compress(_pad_rows(v_cmp, SLC_BLOCK), phi[1], phi_b[1])
    o = _nsa_prompt_attention(q, q_rot, gates, kc, vc, k_slc, v_slc, k_win, v_win)
    n_win = min(WINDOW, s)
    rows = jnp.stack([k_cmp, v_cmp, k_slc, v_slc], axis=2)
    win_rows = jnp.stack([k_win, v_win], axis=2)[:, s - n_win:]
    return o, rows, win_rows


def _nsa_sample(q, kv, gates, phi, phi_b, pool, page_table, win_buf):
    bsz, t = q.shape[0], q.shape[1]
    pos = PAST_LEN + jnp.arange(t)
    k_cmp, v_cmp, k_slc, v_slc, k_win, v_win = [kv[:, :, i] for i in range(6)]
    q_rot, k_slc, k_win = _rope(q, pos), _rope(k_slc, pos), _rope(k_win, pos)
    n_pages = PAST_LEN // PAGE_SIZE
    past = pool[page_table, :, :2].reshape(bsz, n_pages * PAGE_SIZE, 2, NSA_KV_HEADS, HEAD_DIM)
    kc = _compress(_pad_rows(jnp.concatenate([past[:, :, 0], k_cmp], axis=1), SLC_BLOCK), phi[0], phi_b[0])
    vc = _compress(_pad_rows(jnp.concatenate([past[:, :, 1], v_cmp], axis=1), SLC_BLOCK), phi[1], phi_b[1])
    bi = jnp.arange(bsz)[:, None, None, None]
    gi = jnp.arange(NSA_KV_HEADS)[None, None, :, None]

    def fetch(tok):
        tp = jnp.clip(tok, 0, PAST_LEN - 1)
        phys = page_table[bi, tp // PAGE_SIZE]
        off = tp % PAGE_SIZE
        tn = jnp.clip(tok - PAST_LEN, 0, t - 1)
        is_new = (tok >= PAST_LEN)[..., None]
        k_g = jnp.where(is_new, k_slc[bi, tn, gi], pool[phys, off, 2, gi])
        v_g = jnp.where(is_new, v_slc[bi, tn, gi], pool[phys, off, 3, gi])
        return k_g, v_g

    kw = jnp.concatenate([win_buf[:, :, 0].astype(q.dtype), k_win], axis=1)
    vw = jnp.concatenate([win_buf[:, :, 1].astype(q.dtype), v_win], axis=1)
    n_win = win_buf.shape[1]
    kwpos = PAST_LEN - n_win + jnp.arange(n_win + t)
    o = _nsa_core(q, q_rot, pos, kc, vc, fetch, kw, vw, kwpos, gates)
    rows = jnp.stack([k_cmp, v_cmp, k_slc, v_slc], axis=2)
    win_rows = jnp.stack([k_win, v_win], axis=2)
    return o, rows, win_rows


LANES = 128
WKV_T_CHUNK = 64


def _wkv_kernel(w_ref, kk_ref, kka_ref, k_ref, r_ref, v_ref, s0_ref, y_ref, sfin_ref, s_scr):
    c = pl.program_id(0)
    n_vg = s_scr.shape[0]

    @pl.when(c == 0)
    def _():
        s_scr[...] = s0_ref[...]

    def step(t, carry):
        w, kk, kka, k, r = w_ref[t], kk_ref[t], kka_ref[t], k_ref[t], r_ref[t]
        for vg in range(n_vg):
            s = s_scr[vg]
            sa = jnp.sum(s * kk, axis=0, keepdims=True)
            s = s * w - kka * sa + k * v_ref[t, vg:vg + 1, :]
            s_scr[vg] = s
            y_ref[t, vg:vg + 1, :] = jnp.sum(s * r, axis=0, keepdims=True)
        return carry

    lax.fori_loop(0, w_ref.shape[0], step, 0)

    @pl.when(c == pl.num_programs(0) - 1)
    def _():
        sfin_ref[...] = s_scr[...]


def _wkv_scan(r, w, k, v, kk, kka, s0):
    bsz, t, h, n = r.shape
    bh = bsz * h
    vrep = LANES // bh
    n_vg = n // vrep
    tc = min(WKV_T_CHUNK, t)

    def key_tiles(z):
        z = z.transpose(1, 3, 0, 2).reshape(t, n, 1, bh)
        return jnp.broadcast_to(z, (t, n, vrep, bh)).reshape(t, n, LANES)

    v_rows = v.transpose(1, 3, 0, 2).reshape(t, n_vg, LANES)
    s_tiles = s0.transpose(2, 3, 0, 1).reshape(n_vg, vrep, n, bh).transpose(0, 2, 1, 3).reshape(n_vg, n, LANES)
    kspec = pl.BlockSpec((tc, n, LANES), lambda c: (c, 0, 0))
    vspec = pl.BlockSpec((tc, n_vg, LANES), lambda c: (c, 0, 0))
    sspec = pl.BlockSpec((n_vg, n, LANES), lambda c: (0, 0, 0))
    y, s_fin = pl.pallas_call(
        _wkv_kernel,
        grid=(t // tc,),
        in_specs=[kspec] * 5 + [vspec, sspec],
        out_specs=[vspec, sspec],
        out_shape=[jax.ShapeDtypeStruct((t, n_vg, LANES), jnp.float32),
                   jax.ShapeDtypeStruct((n_vg, n, LANES), jnp.float32)],
        scratch_shapes=[pltpu.VMEM((n_vg, n, LANES), jnp.float32)],
        compiler_params=pltpu.CompilerParams(dimension_semantics=("arbitrary",),
                                             vmem_limit_bytes=48 * 1024 * 1024),
        name="wkv_scan",
    )(key_tiles(w), key_tiles(kk), key_tiles(kka), key_tiles(k), key_tiles(r), v_rows, s_tiles)
    y = y.reshape(t, n, bsz, h).transpose(2, 0, 3, 1)
    s_fin = s_fin.reshape(n_vg, n, vrep, bsz, h).transpose(3, 4, 0, 2, 1).reshape(bsz, h, n, n)
    return y, s_fin


def _rwkv7(c, wkv0, shift0, mu, w0, w2, a0, a2, g2, k_k, k_a, r_k, ln_g, ln_b):
    f32 = jnp.float32
    bsz, t, _ = c.shape
    prev = jnp.concatenate([shift0.astype(c.dtype), c[:, :-1]], axis=1)
    cm = c + mu * (prev - c)
    r, k, v, wl, al, gl = jnp.split(cm, RWKV_SPLITS, axis=-1)
    log_w = -jax.nn.softplus(-(w0 + jnp.tanh(wl) @ w2).astype(f32)) - 0.5
    decay = jnp.exp(-jnp.exp(log_w))
    a = jax.nn.sigmoid((a0 + al @ a2).astype(f32))
    g = jax.nn.sigmoid(gl) @ g2

    def heads(z):
        return z.astype(f32).reshape(bsz, t, RWKV_HEADS, RWKV_HEAD_DIM)

    kk = heads(k * k_k)
    kk = kk * lax.rsqrt(jnp.sum(kk * kk, axis=-1, keepdims=True) + 1e-12)
    k = k.astype(f32) * (1.0 + (a - 1.0) * k_a.astype(f32))
    rh, kh, vh, wh, ah = heads(r), heads(k), heads(v), heads(decay), heads(a)

    y, s_fin = _wkv_scan(rh, wh, kh, vh, kk, kk * ah, wkv0.astype(f32))
    mean = jnp.mean(y, axis=-1, keepdims=True)
    var = jnp.mean(jnp.square(y - mean), axis=-1, keepdims=True)
    y = ((y - mean) * lax.rsqrt(var + RWKV_GN_EPS)).reshape(bsz, t, RWKV_WIDTH) * ln_g.astype(f32) + ln_b.astype(f32)
    bonus = (jnp.sum(rh * kh * r_k.astype(f32), axis=-1, keepdims=True) * vh).reshape(bsz, t, RWKV_WIDTH)
    out = ((y + bonus) * g.astype(f32)).astype(c.dtype)
    return out, s_fin.astype(c.dtype), c[:, t - 1:]


def _layer(x, l, P, st, nsa_fn):
    lru_h0, lru_conv0, wkv0, shift0, ffn_conv0 = st
    bsz, t, _ = x.shape
    norms = P['norms'][l]
    h = _rmsnorm(x, norms[0])
    proj = _mm3(h, P['w_in'][l])
    xa, q, kv, nsa_g, rw, mg = jnp.split(proj, IN_SPLITS, axis=-1)
    o_a, lru_h, lru_conv = _rg_lru(xa, lru_h0, lru_conv0, P['lru_conv_w'][l], P['lru_conv_b'][l], P['lru_gate_w'][l], P['lru_gate_b'][l], P['lru_lambda'][l])
    q = q.reshape(bsz, t, NSA_HEADS, HEAD_DIM)
    kv = kv.reshape(bsz, t, 6, NSA_KV_HEADS, HEAD_DIM)
    o_b, nsa_rows, win_rows = nsa_fn(l, q, kv, nsa_g)
    o_c, wkv, shift = _rwkv7(rw, wkv0, shift0, P['rwkv_mu'][l], P['rwkv_w0'][l], P['rwkv_w2'][l], P['rwkv_a0'][l], P['rwkv_a2'][l], P['rwkv_g2'][l], P['rwkv_k_k'][l], P['rwkv_k_a'][l], P['rwkv_r_k'][l], P['rwkv_ln_g'][l], P['rwkv_ln_b'][l])
    g_a, g_b, g_c = jnp.split(jax.nn.sigmoid(mg), 3, axis=-1)
    wb = P['w_branch'][l]
    merged = g_a * _mm3(o_a, wb[0]) + g_b * _mm3(o_b, wb[1]) + g_c * _mm3(o_c, wb[2])
    x = x + _rmsnorm(_mm3(merged, P['w_out'][l]), norms[1])
    u, ffn_conv = _causal_conv(_mm3(_rmsnorm(x, norms[2]), P['w_up'][l]), ffn_conv0, P['ffn_conv_w'][l], P['ffn_conv_b'][l])
    u_gate, u_val = jnp.split(u, 2, axis=-1)
    x = x + _rmsnorm(_mm3(jax.nn.gelu(u_gate) * u_val, P['w_down'][l]), norms[3])
    return x, (nsa_rows, win_rows, lru_h, lru_conv, wkv, shift, ffn_conv)


def kernel(x_prompt, x_sample, cache_nsa, cache_win, state_lru_h, state_lru_conv, state_rwkv_wkv, state_rwkv_shift, state_ffn_conv, page_table, norms, w_in, lru_conv_w, lru_conv_b, lru_gate_w, lru_gate_b, lru_lambda, nsa_phi, nsa_phi_b, rwkv_mu, rwkv_w0, rwkv_w2, rwkv_a0, rwkv_a2, rwkv_g2, rwkv_k_k, rwkv_k_a, rwkv_r_k, rwkv_ln_g, rwkv_ln_b, w_branch, w_out, w_up, ffn_conv_w, ffn_conv_b, w_down):
    P = {'norms': norms, 'w_in': w_in, 'lru_conv_w': lru_conv_w, 'lru_conv_b': lru_conv_b, 'lru_gate_w': lru_gate_w, 'lru_gate_b': lru_gate_b, 'lru_lambda': lru_lambda, 'rwkv_mu': rwkv_mu, 'rwkv_w0': rwkv_w0, 'rwkv_w2': rwkv_w2, 'rwkv_a0': rwkv_a0, 'rwkv_a2': rwkv_a2, 'rwkv_g2': rwkv_g2, 'rwkv_k_k': rwkv_k_k, 'rwkv_k_a': rwkv_k_a, 'rwkv_r_k': rwkv_r_k, 'rwkv_ln_g': rwkv_ln_g, 'rwkv_ln_b': rwkv_ln_b, 'w_branch': w_branch, 'w_out': w_out, 'w_up': w_up, 'ffn_conv_w': ffn_conv_w, 'ffn_conv_b': ffn_conv_b, 'w_down': w_down}

    def nsa_prompt_fn(l, q, kv, g):
        return _nsa_prompt(q, kv, g, nsa_phi[l], nsa_phi_b[l])

    def nsa_sample_fn(l, q, kv, g):
        return _nsa_sample(q, kv, g, nsa_phi[l], nsa_phi_b[l], cache_nsa[l], page_table, cache_win[l])

    bsz, dt = x_prompt.shape[0], x_prompt.dtype
    zero_state = (jnp.zeros((bsz, LRU_WIDTH), dt), jnp.zeros((bsz, LRU_CONV - 1, LRU_WIDTH), dt), jnp.zeros((bsz, RWKV_HEADS, RWKV_HEAD_DIM, RWKV_HEAD_DIM), dt), jnp.zeros((bsz, 1, RWKV_COLS), dt), jnp.zeros((bsz, FFN_CONV - 1, 2 * D_FF), dt))
    y_p, y_s = x_prompt, x_sample
    new_p, new_s = [], []
    for l in range(DEPTH):
        y_p, st_p = _layer(y_p, l, P, zero_state, nsa_prompt_fn)
        y_s, st_s = _layer(y_s, l, P, (state_lru_h[l], state_lru_conv[l], state_rwkv_wkv[l], state_rwkv_shift[l], state_ffn_conv[l]), nsa_sample_fn)
        new_p.append(st_p)
        new_s.append(st_s)

    def stacked(rows, i):
        return jnp.stack([r[i] for r in rows])

    return (y_p, y_s) + tuple(stacked(new_p, i) for i in range(7)) + tuple(stacked(new_s, i) for i in range(7))
```

```python
import functools

import jax
import jax.numpy as jnp
from jax import lax
from jax.experimental import pallas as pl
from jax.experimental.pallas import tpu as pltpu

D_MODEL = 2048
DEPTH = 4
PAST_LEN = 16384
PAGE_SIZE = 128
NORM_EPS = 1e-6
BRANCH_WIDTH = 1024
LRU_WIDTH = BRANCH_WIDTH
LRU_BLOCKS = 16
LRU_BLOCK_DIM = LRU_WIDTH // LRU_BLOCKS
LRU_CONV = 4
LRU_C = 8.0
NSA_HEADS = 16
NSA_KV_HEADS = 4
HEAD_DIM = 64
NSA_WIDTH = NSA_HEADS * HEAD_DIM
CMP_STRIDE = 16
CMP_BLOCK = 2 * CMP_STRIDE
SLC_BLOCK = 64
N_SELECT = 16
WINDOW = 512
Q_BLOCK = 128
ROPE_THETA = 10000.0
FORCE_BONUS = 1000.0
NEG_INF = -1e30
RWKV_HEADS = 16
RWKV_HEAD_DIM = 64
RWKV_WIDTH = RWKV_HEADS * RWKV_HEAD_DIM
DECAY_LORA = 64
ICL_LORA = 64
GATE_LORA = 160
RWKV_GN_EPS = 64e-5
D_FF = 3 * D_MODEL
FFN_CONV = 3
N_KV_COLS = 6 * NSA_KV_HEADS * HEAD_DIM
N_NSA_GATES = 3 * NSA_HEADS
RWKV_COLS = 3 * RWKV_WIDTH + DECAY_LORA + ICL_LORA + GATE_LORA
IN_SPLITS = (LRU_WIDTH, LRU_WIDTH + NSA_WIDTH, LRU_WIDTH + NSA_WIDTH + N_KV_COLS,
             LRU_WIDTH + NSA_WIDTH + N_KV_COLS + N_NSA_GATES,
             LRU_WIDTH + NSA_WIDTH + N_KV_COLS + N_NSA_GATES + RWKV_COLS)
D_IN = IN_SPLITS[-1] + 3 * D_MODEL
RWKV_SPLITS = (RWKV_WIDTH, 2 * RWKV_WIDTH, 3 * RWKV_WIDTH, 3 * RWKV_WIDTH + DECAY_LORA,
               3 * RWKV_WIDTH + DECAY_LORA + ICL_LORA)


def _mm_kernel(x_ref, w_ref, o_ref):
    @pl.when(pl.program_id(2) == 0)
    def _():
        o_ref[...] = jnp.zeros_like(o_ref)

    o_ref[...] += jnp.dot(x_ref[...].astype(jnp.bfloat16), w_ref[...].astype(jnp.bfloat16),
                          preferred_element_type=jnp.float32)


def _mm(x, w, tm=512, tn=512, tk=1024):
    m, k = x.shape
    n = w.shape[1]
    tm = min(tm, m)
    tk = min(tk, k)
    assert m % tm == 0 and k % tk == 0
    return pl.pallas_call(
        _mm_kernel,
        grid=(m // tm, pl.cdiv(n, tn), k // tk),
        in_specs=[pl.BlockSpec((tm, tk), lambda i, j, kk: (i, kk)),
                  pl.BlockSpec((tk, tn), lambda i, j, kk: (kk, j))],
        out_specs=pl.BlockSpec((tm, tn), lambda i, j, kk: (i, j)),
        out_shape=jax.ShapeDtypeStruct((m, n), jnp.float32),
        compiler_params=pltpu.CompilerParams(
            dimension_semantics=("parallel", "parallel", "arbitrary")),
        name="mm",
    )(x, w)


def _mm3(x, w):
    b, t, k = x.shape
    return _mm(x.reshape(b * t, k), w).reshape(b, t, w.shape[1])


VMEM_LIMIT = 56 * 1024 * 1024
BF16 = jnp.bfloat16


def _params(*sem):
    return pltpu.CompilerParams(dimension_semantics=sem, vmem_limit_bytes=VMEM_LIMIT)


def _mm_ws(x, w, l, tn, tm=1024):
    m, k = x.shape
    n = w.shape[2]
    tm = min(tm, m)
    cast = w.dtype != BF16

    def kern(x_ref, w_ref, o_ref, *scr):
        if cast:
            @pl.when(pl.program_id(1) == 0)
            def _():
                scr[0][...] = w_ref[...].astype(BF16)
            wv = scr[0][...]
        else:
            wv = w_ref[...]
        o_ref[...] = jnp.dot(x_ref[...], wv, preferred_element_type=jnp.float32)

    return pl.pallas_call(
        kern,
        grid=(n // tn, m // tm),
        in_specs=[pl.BlockSpec((tm, k), lambda j, i: (i, 0)),
                  pl.BlockSpec((None, k, tn), lambda j, i: (l, 0, j))],
        out_specs=pl.BlockSpec((tm, tn), lambda j, i: (i, j)),
        out_shape=jax.ShapeDtypeStruct((m, n), jnp.float32),
        scratch_shapes=[pltpu.VMEM((k, tn), BF16)] if cast else [],
        compiler_params=_params("arbitrary", "arbitrary"),
        name="mm_ws",
    )(x, w)


def _branch_merge(oa, ob, oc, wb, l, mg, tn=512, tm=1024):
    m, kb = oa.shape
    n = wb.shape[3]
    tm = min(tm, m)
    nj = n // tn

    def kern(a_ref, b_ref, c_ref, wa_ref, wb_ref, wc_ref, ga_ref, gb_ref, gc_ref, o_ref, sa, sb, sc):
        @pl.when(pl.program_id(1) == 0)
        def _():
            sa[...] = wa_ref[...].astype(BF16)
            sb[...] = wb_ref[...].astype(BF16)
            sc[...] = wc_ref[...].astype(BF16)

        def term(x_ref, s_ref, g_ref):
            return jax.nn.sigmoid(g_ref[...]) * jnp.dot(x_ref[...], s_ref[...], preferred_element_type=jnp.float32)

        o_ref[...] = (term(a_ref, sa, ga_ref) + term(b_ref, sb, gb_ref) + term(c_ref, sc, gc_ref)).astype(BF16)

    xspec = pl.BlockSpec((tm, kb), lambda j, i: (i, 0))

    def wspec(br):
        return pl.BlockSpec((None, None, kb, tn), lambda j, i: (l, br, 0, j))

    def gspec(br):
        return pl.BlockSpec((tm, tn), lambda j, i: (i, br * nj + j))

    return pl.pallas_call(
        kern,
        grid=(nj, m // tm),
        in_specs=[xspec, xspec, xspec, wspec(0), wspec(1), wspec(2), gspec(0), gspec(1), gspec(2)],
        out_specs=pl.BlockSpec((tm, tn), lambda j, i: (i, j)),
        out_shape=jax.ShapeDtypeStruct((m, n), BF16),
        scratch_shapes=[pltpu.VMEM((kb, tn), BF16)] * 3,
        compiler_params=_params("arbitrary", "arbitrary"),
        name="branch_merge",
    )(oa, ob, oc, wb, wb, wb, mg, mg, mg)


def _mm_norm_res(x, w, l, resid, g_post, g_next, tm=512, tk=1024):
    m, k = x.shape
    n = w.shape[2]
    tm = min(tm, m)
    nk = k // tk

    def kern(x_ref, w_ref, r_ref, gp_ref, gn_ref, xo_ref, ho_ref, acc):
        kk = pl.program_id(1)

        @pl.when(kk == 0)
        def _():
            acc[...] = jnp.zeros_like(acc)

        acc[...] += jnp.dot(x_ref[...], w_ref[...], preferred_element_type=jnp.float32)

        @pl.when(kk == nk - 1)
        def _():
            z = acc[...]
            y = z * lax.rsqrt(jnp.mean(z * z, axis=-1, keepdims=True) + NORM_EPS) * gp_ref[...]
            xn = r_ref[...] + y
            xo_ref[...] = xn
            hn = xn * lax.rsqrt(jnp.mean(xn * xn, axis=-1, keepdims=True) + NORM_EPS) * gn_ref[...]
            ho_ref[...] = hn.astype(BF16)

    row = pl.BlockSpec((tm, n), lambda i, kk: (i, 0))
    gain = pl.BlockSpec((1, n), lambda i, kk: (0, 0))
    return pl.pallas_call(
        kern,
        grid=(m // tm, nk),
        in_specs=[pl.BlockSpec((tm, tk), lambda i, kk: (i, kk)),
                  pl.BlockSpec((None, tk, n), lambda i, kk: (l, kk, 0)),
                  row, gain, gain],
        out_specs=[row, row],
        out_shape=[jax.ShapeDtypeStruct((m, n), jnp.float32), jax.ShapeDtypeStruct((m, n), BF16)],
        scratch_shapes=[pltpu.VMEM((tm, n), jnp.float32)],
        compiler_params=_params("arbitrary", "arbitrary"),
        name="mm_norm_res",
    )(x, w, resid, g_post, g_next)


def _ffn_up_act(h, w_up, l, conv_w, conv_b, conv0, t_len, tm=512, tn=512):
    m, k = h.shape
    f = conv_w.shape[2]
    bsz = m // t_len
    nj = f // tn
    tpb = t_len // tm
    taps = FFN_CONV - 1

    def kern(h_ref, wg_ref, wv_ref, cw_ref, cb_ref, c0_ref, act_ref, st_ref, sg, sv, pg, pv):
        i = pl.program_id(1)

        @pl.when(i == 0)
        def _():
            sg[...] = wg_ref[...].astype(BF16)
            sv[...] = wv_ref[...].astype(BF16)

        @pl.when(i % tpb == 0)
        def _():
            pg[...] = c0_ref[:, 0, :]
            pv[...] = c0_ref[:, 1, :]

        rid = lax.broadcasted_iota(jnp.int32, (tm, tn), 0)

        def conv(u, prev, half):
            u1 = jnp.where(rid == 0, prev[1:2], pltpu.roll(u, 1, axis=0))
            u2 = jnp.where(rid == 0, prev[0:1], jnp.where(rid == 1, prev[1:2], pltpu.roll(u, 2, axis=0)))
            return (cb_ref[half:half + 1, :] + u2 * cw_ref[0, half:half + 1, :]
                    + u1 * cw_ref[1, half:half + 1, :] + u * cw_ref[2, half:half + 1, :])

        ug = jnp.dot(h_ref[...], sg[...], preferred_element_type=jnp.float32)
        uv = jnp.dot(h_ref[...], sv[...], preferred_element_type=jnp.float32)
        act_ref[...] = (jax.nn.gelu(conv(ug, pg[...], 0)) * conv(uv, pv[...], 1)).astype(BF16)
        pg[...] = ug[tm - taps:]
        pv[...] = uv[tm - taps:]

        @pl.when(i % tpb == tpb - 1)
        def _():
            st_ref[:, 0, :] = ug[tm - taps:]
            st_ref[:, 1, :] = uv[tm - taps:]

    def wspec(off):
        return pl.BlockSpec((None, k, tn), lambda j, i: (l, 0, j + off))

    stspec = pl.BlockSpec((None, taps, 2, tn), lambda j, i: (i // tpb, 0, 0, j))
    return pl.pallas_call(
        kern,
        grid=(nj, m // tm),
        in_specs=[pl.BlockSpec((tm, k), lambda j, i: (i, 0)), wspec(0), wspec(nj),
                  pl.BlockSpec((FFN_CONV, 2, tn), lambda j, i: (0, 0, j)),
                  pl.BlockSpec((2, tn), lambda j, i: (0, j)), stspec],
        out_specs=[pl.BlockSpec((tm, tn), lambda j, i: (i, j)), stspec],
        out_shape=[jax.ShapeDtypeStruct((m, f), BF16), jax.ShapeDtypeStruct((bsz, taps, 2, f), jnp.float32)],
        scratch_shapes=[pltpu.VMEM((k, tn), BF16), pltpu.VMEM((k, tn), BF16),
                        pltpu.VMEM((taps, tn), jnp.float32), pltpu.VMEM((taps, tn), jnp.float32)],
        compiler_params=_params("arbitrary", "arbitrary"),
        name="ffn_up_act",
    )(h, w_up, w_up, conv_w, conv_b, conv0)


PAGES_PER_STEP = 8


def _past_compress(pool, l, page_table, w2):
    bsz, n_pages = page_table.shape
    cpp = PAGE_SIZE // CMP_STRIDE
    n_steps = n_pages // PAGES_PER_STEP
    width = 2 * NSA_KV_HEADS * HEAD_DIM
    pairs = width // (2 * HEAD_DIM)
    rows = PAGES_PER_STEP * cpp

    def kern(pt_ref, *refs):
        pages = refs[:PAGES_PER_STEP]
        w_ref, o_ref = refs[PAGES_PER_STEP], refs[PAGES_PER_STEP + 1]
        x = jnp.concatenate([p[...] for p in pages], axis=0).reshape(rows, CMP_STRIDE, width)
        acc = [jnp.zeros((rows, 4 * HEAD_DIM), jnp.float32) for _ in range(pairs)]
        for j in range(CMP_STRIDE):
            xj = x[:, j, :].astype(BF16)
            for pr in range(pairs):
                kind = pr // (pairs // 2)
                acc[pr] = acc[pr] + jnp.dot(xj[:, pr * 2 * HEAD_DIM:(pr + 1) * 2 * HEAD_DIM], w_ref[kind, j],
                                            preferred_element_type=jnp.float32)
        for pr in range(pairs):
            o_ref[:, pr * 4 * HEAD_DIM:(pr + 1) * 4 * HEAD_DIM] = acc[pr]

    def page_spec(jj):
        return pl.BlockSpec((None, None, PAGE_SIZE, width),
                            lambda b, s, pt: (l, pt[b, s * PAGES_PER_STEP + jj], 0, 0))

    return pl.pallas_call(
        kern,
        grid_spec=pltpu.PrefetchScalarGridSpec(
            num_scalar_prefetch=1,
            grid=(bsz, n_steps),
            in_specs=[page_spec(jj) for jj in range(PAGES_PER_STEP)]
            + [pl.BlockSpec(w2.shape, lambda b, s, pt: (0, 0, 0, 0))],
            out_specs=pl.BlockSpec((None, rows, 2 * width), lambda b, s, pt: (b, s, 0)),
        ),
        out_shape=jax.ShapeDtypeStruct((bsz, n_pages * cpp, 2 * width), jnp.float32),
        compiler_params=_params("arbitrary", "arbitrary"),
        name="past_compress",
    )(page_table, *([pool] * PAGES_PER_STEP), w2)


def _rmsnorm(x, g):
    xf = x.astype(jnp.float32)
    y = xf * lax.rsqrt(jnp.mean(xf * xf, axis=-1, keepdims=True) + NORM_EPS)
    return (y * g.astype(jnp.float32)).astype(x.dtype)


def _causal_conv(x, buf, w, b):
    k, t = w.shape[0], x.shape[1]
    xp = jnp.concatenate([buf.astype(x.dtype), x], axis=1)
    y = b + xp[:, 0:t] * w[0]
    for j in range(1, k):
        y = y + xp[:, j:j + t] * w[j]
    return y, xp[:, xp.shape[1] - (k - 1):]


def _rope(x, pos):
    half = x.shape[-1] // 2
    inv = ROPE_THETA ** (-jnp.arange(half, dtype=jnp.float32) / half)
    ang = pos.astype(jnp.float32)[:, None] * inv[None, :]
    cos = jnp.cos(ang)[None, :, None, :].astype(x.dtype)
    sin = jnp.sin(ang)[None, :, None, :].astype(x.dtype)
    x1, x2 = x[..., :half], x[..., half:]
    return jnp.concatenate([x1 * cos - x2 * sin, x2 * cos + x1 * sin], axis=-1)


def _masked_softmax(s, mask):
    p = jax.nn.softmax(jnp.where(mask, s, NEG_INF), axis=-1)
    return jnp.where(mask, p, 0.0)


def _lin_combine(left, right):
    a1, b1 = left
    a2, b2 = right
    return a1 * a2, a2 * b1 + b2


def _pad_rows(z, mult):
    pad = -z.shape[1] % mult
    return jnp.pad(z, ((0, 0), (0, pad)) + ((0, 0),) * (z.ndim - 2))


def _rg_lru(xa, h0, conv_buf, conv_w, conv_b, gate_w, gate_b, lam):
    f32 = jnp.float32
    bsz, t, _ = xa.shape
    xc, new_buf = _causal_conv(xa, conv_buf, conv_w, conv_b)
    xb = xc.reshape(bsz, t, LRU_BLOCKS, LRU_BLOCK_DIM)
    gates = jnp.einsum('btnd,gnde->gbtne', xb, gate_w).reshape(2, bsz, t, LRU_WIDTH)
    gates = gates.astype(f32) + gate_b.astype(f32)[:, None, None, :]
    r, i = jax.nn.sigmoid(gates[0]), jax.nn.sigmoid(gates[1])
    log_a = -LRU_C * r * jax.nn.softplus(-lam.astype(f32))
    a = jnp.exp(log_a)
    b = jnp.sqrt(-jnp.expm1(2.0 * log_a)) * (i * xc.astype(f32))
    b = b.at[:, 0].add(a[:, 0] * h0.astype(f32))
    _, h = lax.associative_scan(_lin_combine, (a, b), axis=1)
    return h.astype(xa.dtype), h[:, -1].astype(xa.dtype), new_buf


def _compress(z, w, b):
    bsz, length, g, d = z.shape
    ch = z.reshape(bsz, length // CMP_STRIDE, CMP_STRIDE, g, d)
    head = jnp.einsum('bcjgd,jde->bcge', ch, w[:CMP_STRIDE])
    tail = jnp.einsum('bcjgd,jde->bcge', ch, w[CMP_STRIDE:])
    return head[:, :-1] + tail[:, 1:] + b


def _nsa_core(q, q_rot, qpos, kc, vc, fetch, kw, vw, kwpos, gates):
    f32 = jnp.float32
    bsz, t = q.shape[0], q.shape[1]
    hpg = NSA_HEADS // NSA_KV_HEADS
    scale = HEAD_DIM ** -0.5
    qg = q.reshape(bsz, t, NSA_KV_HEADS, hpg, HEAD_DIM)
    qrg = q_rot.reshape(bsz, t, NSA_KV_HEADS, hpg, HEAD_DIM)
    n_cmp = kc.shape[1]
    cmp_end = jnp.arange(n_cmp) * CMP_STRIDE + (CMP_BLOCK - 1)
    m_c = (cmp_end[None, :] <= qpos[:, None])[None, :, None, None, :]
    p_c = _masked_softmax(jnp.einsum('btghd,bngd->btghn', qg, kc).astype(f32) * scale, m_c)
    o_c = jnp.einsum('btghn,bngd->btghd', p_c.astype(vc.dtype), vc)
    per = SLC_BLOCK // CMP_STRIDE
    n_slc = (n_cmp + 1) // per
    imp = jnp.pad(p_c.sum(axis=3), ((0, 0), (0, 0), (0, 0), (0, 1))).reshape(bsz, t, NSA_KV_HEADS, n_slc, per)
    imp = imp.sum(-1) + jnp.pad(imp[..., :-1, per - 1], ((0, 0), (0, 0), (0, 0), (1, 0)))
    blk = jnp.arange(n_slc)[None, :]
    qblk = (qpos // SLC_BLOCK)[:, None]
    valid = blk * SLC_BLOCK <= qpos[:, None]
    forced = (blk == 0) | (blk == qblk) | (blk == qblk - 1)
    score = jnp.where(valid[None, :, None, :], imp + FORCE_BONUS * forced[None, :, None, :], NEG_INF)
    n_top = min(N_SELECT, n_slc)
    _, idx = lax.top_k(score, n_top)
    tok = (idx[..., None] * SLC_BLOCK + jnp.arange(SLC_BLOCK)).reshape(bsz, t, NSA_KV_HEADS, n_top * SLC_BLOCK)
    m_s = (tok <= qpos[None, :, None, None])[:, :, :, None, :]
    k_sel, v_sel = fetch(tok)
    p_s = _masked_softmax(jnp.einsum('btghd,btgkd->btghk', qrg, k_sel).astype(f32) * scale, m_s)
    o_s = jnp.einsum('btghk,btgkd->btghd', p_s.astype(v_sel.dtype), v_sel)
    dpos = qpos[:, None] - kwpos[None, :]
    m_w = ((dpos >= 0) & (dpos < WINDOW) & (kwpos[None, :] >= 0))[None, :, None, None, :]
    p_w = _masked_softmax(jnp.einsum('btghd,bkgd->btghk', qrg, kw).astype(f32) * scale, m_w)
    o_w = jnp.einsum('btghk,bkgd->btghd', p_w.astype(vw.dtype), vw)
    gt = jax.nn.sigmoid(gates.astype(f32)).astype(q.dtype).reshape(bsz, t, NSA_KV_HEADS, hpg, 3, 1)
    o = gt[..., 0, :] * o_c + gt[..., 1, :] * o_s + gt[..., 2, :] * o_w
    return o.reshape(bsz, t, NSA_WIDTH)


HPG = NSA_HEADS // NSA_KV_HEADS
CMP_PER_SLC = SLC_BLOCK // CMP_STRIDE
SEL_CHUNK = 512
WIN_KEYS = WINDOW + Q_BLOCK
DROPPED = -3e38


def _softmax_rows(s, ok):
    sm = jnp.where(ok[None], s, NEG_INF)
    e = jnp.exp(sm - jnp.max(sm, axis=-1, keepdims=True))
    p = e / jnp.sum(e, axis=-1, keepdims=True)
    return jnp.where(ok[None], p, 0.0)


def _nsa_prompt_kernel(q_ref, qr_ref, g_ref, kct_ref, vc_ref, kst_ref, vs_ref, kwt_ref, vw_ref,
                       impt_ref, exp_ref, o_ref, *, n_cmp):
    f32, bf = jnp.float32, jnp.bfloat16
    i = pl.program_id(2)
    start = i * Q_BLOCK
    rows = HPG * Q_BLOCK
    n_cpad = kct_ref.shape[-1]
    n_slc = impt_ref.shape[0]
    scale = HEAD_DIM ** -0.5
    q = (q_ref[0, 0, 0] * scale).astype(bf)
    qr = (qr_ref[0, 0, 0] * scale).astype(bf)
    t_col = start + lax.broadcasted_iota(jnp.int32, (Q_BLOCK, 1), 0)

    s = jnp.dot(q, kct_ref[0, 0], preferred_element_type=f32).reshape(HPG, Q_BLOCK, n_cpad)
    n_io = lax.broadcasted_iota(jnp.int32, (Q_BLOCK, n_cpad), 1)
    ok_c = jnp.where(n_io < n_cmp, n_io * CMP_STRIDE + (CMP_BLOCK - 1), 2 ** 30) <= t_col
    p = _softmax_rows(s, ok_c)
    o_c = jnp.dot(p.reshape(rows, n_cpad).astype(bf), vc_ref[0, 0], preferred_element_type=f32)

    psum = p[0] + p[1] + p[2] + p[3]
    hi = psum.astype(bf)
    r1 = psum - hi.astype(f32)
    mid = r1.astype(bf)
    lo = (r1 - mid.astype(f32)).astype(bf)
    nt = (((1,), (1,)), ((), ()))
    imp_t = (lax.dot_general(impt_ref[...], hi, nt, preferred_element_type=f32)
             + lax.dot_general(impt_ref[...], mid, nt, preferred_element_type=f32)
             + lax.dot_general(impt_ref[...], lo, nt, preferred_element_type=f32))
    j_io = lax.broadcasted_iota(jnp.int32, (n_slc, Q_BLOCK), 0)
    t_row = start + lax.broadcasted_iota(jnp.int32, (n_slc, Q_BLOCK), 1)
    qblk = t_row // SLC_BLOCK
    forced = jnp.where(j_io == 0, 1.0, 0.0) + jnp.where(j_io == qblk, 1.0, 0.0) + jnp.where(j_io == qblk - 1, 1.0, 0.0)
    forced = jnp.minimum(forced, 1.0)
    score = jnp.where(j_io * SLC_BLOCK <= t_row, imp_t + FORCE_BONUS * forced, NEG_INF)

    def pick(_, carry):
        sc, sel = carry
        best = jnp.max(sc, axis=0, keepdims=True)
        first = jnp.min(jnp.where(sc == best, j_io, n_slc), axis=0, keepdims=True)
        hit = j_io == first
        return jnp.where(hit, DROPPED, sc), jnp.where(hit, 1.0, sel)

    _, sel_t = lax.fori_loop(0, min(N_SELECT, n_slc), pick, (score, jnp.zeros((n_slc, Q_BLOCK), f32)))
    sel = sel_t.T.astype(bf)

    def chunk(c, carry):
        m, l, acc = carry
        off = pl.multiple_of(c * SEL_CHUNK, SEL_CHUNK)
        kt = kst_ref[0, 0, :, pl.ds(off, SEL_CHUNK)]
        v = vs_ref[0, 0, pl.ds(off, SEL_CHUNK), :]
        sc = jnp.dot(qr, kt, preferred_element_type=f32).reshape(HPG, Q_BLOCK, SEL_CHUNK)
        chosen = jnp.dot(sel, exp_ref[:, pl.ds(off, SEL_CHUNK)], preferred_element_type=f32)
        kpos = off + lax.broadcasted_iota(jnp.int32, (Q_BLOCK, SEL_CHUNK), 1)
        ok = jnp.where(kpos <= t_col, chosen, 0.0) > 0.5
        sc = jnp.where(ok[None], sc, NEG_INF)
        m_new = jnp.maximum(m, jnp.max(sc, axis=-1, keepdims=True))
        alpha = jnp.exp(m - m_new)
        pe = jnp.exp(sc - m_new)
        l = alpha * l + jnp.sum(pe, axis=-1, keepdims=True)
        pv = jnp.dot(pe.reshape(rows, SEL_CHUNK).astype(bf), v, preferred_element_type=f32)
        return m_new, l, alpha * acc + pv.reshape(HPG, Q_BLOCK, HEAD_DIM)

    n_chunks = (start + Q_BLOCK + SEL_CHUNK - 1) // SEL_CHUNK
    init = (jnp.full((HPG, Q_BLOCK, 1), NEG_INF, f32), jnp.zeros((HPG, Q_BLOCK, 1), f32),
            jnp.zeros((HPG, Q_BLOCK, HEAD_DIM), f32))
    _, l_s, acc_s = lax.fori_loop(0, n_chunks, chunk, init)
    o_s = (acc_s / l_s).reshape(rows, HEAD_DIM)

    n_keys = kwt_ref.shape[-1]
    wk = min(WIN_KEYS, n_keys)
    k0 = pl.multiple_of(jnp.maximum(start + Q_BLOCK - wk, 0), Q_BLOCK)
    kt = kwt_ref[0, 0, :, pl.ds(k0, wk)]
    v = vw_ref[0, 0, pl.ds(k0, wk), :]
    sw = jnp.dot(qr, kt, preferred_element_type=f32).reshape(HPG, Q_BLOCK, wk)
    dpos = t_col - (k0 + lax.broadcasted_iota(jnp.int32, (Q_BLOCK, wk), 1))
    ok_w = jnp.where(dpos >= 0, dpos, WINDOW) < WINDOW
    pw = _softmax_rows(sw, ok_w)
    o_w = jnp.dot(pw.reshape(rows, wk).astype(bf), v, preferred_element_type=f32)

    gt = jax.nn.sigmoid(g_ref[0, 0, 0])
    o_ref[0, 0, 0] = gt[:, 0:1] * o_c + gt[:, 1:2] * o_s + gt[:, 2:3] * o_w


def _nsa_prompt_attention(q, q_rot, gates, kc, vc, k_slc, v_slc, k_win, v_win):
    bf = jnp.bfloat16
    bsz, s = q.shape[0], q.shape[1]
    g, d = NSA_KV_HEADS, HEAD_DIM
    n_qb = s // Q_BLOCK
    n_cmp = kc.shape[1]
    n_cpad = -(-(n_cmp + 1) // 128) * 128
    n_slc = (n_cmp + 1) // CMP_PER_SLC
    rows = HPG * Q_BLOCK

    def by_block(z, last):
        z = z.reshape(bsz, n_qb, Q_BLOCK, g, HPG, last)
        return z.transpose(0, 3, 1, 4, 2, 5).reshape(bsz, g, n_qb, rows, last)

    def keys_t(z):
        return z.transpose(0, 2, 3, 1).astype(bf)

    def vals(z):
        return z.transpose(0, 2, 1, 3).astype(bf)

    pad_c = ((0, 0), (0, n_cpad - n_cmp), (0, 0), (0, 0))
    n_io = jnp.arange(n_cpad)[None, :]
    j_io = jnp.arange(n_slc)[:, None]
    imp_t = ((n_io >= CMP_PER_SLC * j_io - 1) & (n_io < CMP_PER_SLC * (j_io + 1))).astype(bf)
    expand = (jnp.arange(s)[None, :] // SLC_BLOCK == j_io).astype(bf)

    qspec = pl.BlockSpec((1, 1, 1, rows, d), lambda b, gg, i: (b, gg, i, 0, 0))

    def whole(shape):
        return pl.BlockSpec((1, 1) + shape, lambda b, gg, i: (b, gg, 0, 0))

    out = pl.pallas_call(
        functools.partial(_nsa_prompt_kernel, n_cmp=n_cmp),
        grid=(bsz, g, n_qb),
        in_specs=[qspec, qspec,
                  pl.BlockSpec((1, 1, 1, rows, 3), lambda b, gg, i: (b, gg, i, 0, 0)),
                  whole((d, n_cpad)), whole((n_cpad, d)),
                  whole((d, s)), whole((s, d)), whole((d, s)), whole((s, d)),
                  pl.BlockSpec((n_slc, n_cpad), lambda b, gg, i: (0, 0)),
                  pl.BlockSpec((n_slc, s), lambda b, gg, i: (0, 0))],
        out_specs=qspec,
        out_shape=jax.ShapeDtypeStruct((bsz, g, n_qb, rows, d), jnp.float32),
        compiler_params=pltpu.CompilerParams(
            dimension_semantics=("parallel", "parallel", "arbitrary"),
            vmem_limit_bytes=48 * 1024 * 1024),
        name="nsa_prompt",
    )(by_block(q.reshape(bsz, s, -1), d), by_block(q_rot.reshape(bsz, s, -1), d), by_block(gates, 3),
      keys_t(jnp.pad(kc, pad_c)), vals(jnp.pad(vc, pad_c)),
      keys_t(k_slc), vals(v_slc), keys_t(k_win), vals(v_win), imp_t, expand)
    out = out.reshape(bsz, g, n_qb, HPG, Q_BLOCK, d).transpose(0, 2, 4, 1, 3, 5)
    return out.reshape(bsz, s, NSA_WIDTH)


def _nsa_prompt(q, kv, gates, phi, phi_b):
    bsz, s = q.shape[0], q.shape[1]
    pos = jnp.arange(s)
    k_cmp, v_cmp, k_slc, v_slc, k_win, v_win = [kv[:, :, i] for i in range(6)]
    q_rot, k_slc, k_win = _rope(q, pos), _rope(k_slc, pos), _rope(k_win, pos)
    kc = _compress(_pad_rows(k_cmp, SLC_BLOCK), phi[0], phi_b[0])
    vc = _compress(_pad_rows(v_cmp, SLC_BLOCK), phi[1], phi_b[1])
    o = _nsa_prompt_attention(q, q_rot, gates, kc, vc, k_slc, v_slc, k_win, v_win)
    n_win = min(WINDOW, s)
    rows = jnp.stack([k_cmp, v_cmp, k_slc, v_slc], axis=2)
    win_rows = jnp.stack([k_win, v_win], axis=2)[:, s - n_win:]
    return o, rows, win_rows


def _pair_weights(phi):
    d = HEAD_DIM
    ht = jnp.concatenate([phi[:, :CMP_STRIDE], phi[:, CMP_STRIDE:]], axis=-1)
    z = jnp.zeros_like(ht)
    return jnp.concatenate([jnp.concatenate([ht, z], axis=-1), jnp.concatenate([z, ht], axis=-1)], axis=-2).astype(BF16)


def _sample_compressed(cache, l, page_table, new_rows, phi, phi_b):
    bsz = page_table.shape[0]
    g, d = NSA_KV_HEADS, HEAD_DIM
    pool = cache.reshape(cache.shape[0], cache.shape[1], PAGE_SIZE, 4 * g * d)
    ht = _past_compress(pool, l, page_table, _pair_weights(phi))
    ht = ht.reshape(bsz, ht.shape[1], 2, g, 2, d)
    out = []
    for kind in range(2):
        ch = _pad_rows(new_rows[kind], SLC_BLOCK)
        ch = ch.reshape(bsz, ch.shape[1] // CMP_STRIDE, CMP_STRIDE, g, d)
        head = jnp.concatenate([ht[:, :, kind, :, 0], jnp.einsum('bcjgd,jde->bcge', ch, phi[kind, :CMP_STRIDE])], axis=1)
        tail = jnp.concatenate([ht[:, :, kind, :, 1], jnp.einsum('bcjgd,jde->bcge', ch, phi[kind, CMP_STRIDE:])], axis=1)
        out.append(head[:, :-1] + tail[:, 1:] + phi_b[kind])
    return out


def _nsa_sample(q, kv, gates, phi, phi_b, cache, l, page_table, win_buf):
    bsz, t = q.shape[0], q.shape[1]
    pos = PAST_LEN + jnp.arange(t)
    k_cmp, v_cmp, k_slc, v_slc, k_win, v_win = [kv[:, :, i] for i in range(6)]
    q_rot, k_slc, k_win = _rope(q, pos), _rope(k_slc, pos), _rope(k_win, pos)
    kc, vc = _sample_compressed(cache, l, page_table, (k_cmp, v_cmp), phi, phi_b)
    pool = cache[l]
    bi = jnp.arange(bsz)[:, None, None, None]
    gi = jnp.arange(NSA_KV_HEADS)[None, None, :, None]

    def fetch(tok):
        tp = jnp.clip(tok, 0, PAST_LEN - 1)
        phys = page_table[bi, tp // PAGE_SIZE]
        off = tp % PAGE_SIZE
        tn = jnp.clip(tok - PAST_LEN, 0, t - 1)
        is_new = (tok >= PAST_LEN)[..., None]
        k_g = jnp.where(is_new, k_slc[bi, tn, gi], pool[phys, off, 2, gi])
        v_g = jnp.where(is_new, v_slc[bi, tn, gi], pool[phys, off, 3, gi])
        return k_g, v_g

    kw = jnp.concatenate([win_buf[:, :, 0].astype(q.dtype), k_win], axis=1)
    vw = jnp.concatenate([win_buf[:, :, 1].astype(q.dtype), v_win], axis=1)
    n_win = win_buf.shape[1]
    kwpos = PAST_LEN - n_win + jnp.arange(n_win + t)
    o = _nsa_core(q, q_rot, pos, kc, vc, fetch, kw, vw, kwpos, gates)
    rows = jnp.stack([k_cmp, v_cmp, k_slc, v_slc], axis=2)
    win_rows = jnp.stack([k_win, v_win], axis=2)
    return o, rows, win_rows


LANES = 128
WKV_T_CHUNK = 64


def _wkv_kernel(w_ref, kk_ref, kka_ref, k_ref, r_ref, v_ref, s0_ref, y_ref, sfin_ref, s_scr):
    c = pl.program_id(0)
    n_vg = s_scr.shape[0]

    @pl.when(c == 0)
    def _():
        s_scr[...] = s0_ref[...]

    def step(t, carry):
        w, kk, kka, k, r = w_ref[t], kk_ref[t], kka_ref[t], k_ref[t], r_ref[t]
        for vg in range(n_vg):
            s = s_scr[vg]
            sa = jnp.sum(s * kk, axis=0, keepdims=True)
            s = s * w - kka * sa + k * v_ref[t, vg:vg + 1, :]
            s_scr[vg] = s
            y_ref[t, vg:vg + 1, :] = jnp.sum(s * r, axis=0, keepdims=True)
        return carry

    lax.fori_loop(0, w_ref.shape[0], step, 0)

    @pl.when(c == pl.num_programs(0) - 1)
    def _():
        sfin_ref[...] = s_scr[...]


def _wkv_scan(r, w, k, v, kk, kka, s0):
    bsz, t, h, n = r.shape
    bh = bsz * h
    vrep = LANES // bh
    n_vg = n // vrep
    tc = min(WKV_T_CHUNK, t)

    def key_tiles(z):
        z = z.transpose(1, 3, 0, 2).reshape(t, n, 1, bh)
        return jnp.broadcast_to(z, (t, n, vrep, bh)).reshape(t, n, LANES)

    v_rows = v.transpose(1, 3, 0, 2).reshape(t, n_vg, LANES)
    s_tiles = s0.transpose(2, 3, 0, 1).reshape(n_vg, vrep, n, bh).transpose(0, 2, 1, 3).reshape(n_vg, n, LANES)
    kspec = pl.BlockSpec((tc, n, LANES), lambda c: (c, 0, 0))
    vspec = pl.BlockSpec((tc, n_vg, LANES), lambda c: (c, 0, 0))
    sspec = pl.BlockSpec((n_vg, n, LANES), lambda c: (0, 0, 0))
    y, s_fin = pl.pallas_call(
        _wkv_kernel,
        grid=(t // tc,),
        in_specs=[kspec] * 5 + [vspec, sspec],
        out_specs=[vspec, sspec],
        out_shape=[jax.ShapeDtypeStruct((t, n_vg, LANES), jnp.float32),
                   jax.ShapeDtypeStruct((n_vg, n, LANES), jnp.float32)],
        scratch_shapes=[pltpu.VMEM((n_vg, n, LANES), jnp.float32)],
        compiler_params=pltpu.CompilerParams(dimension_semantics=("arbitrary",),
                                             vmem_limit_bytes=48 * 1024 * 1024),
        name="wkv_scan",
    )(key_tiles(w), key_tiles(kk), key_tiles(kka), key_tiles(k), key_tiles(r), v_rows, s_tiles)
    y = y.reshape(t, n, bsz, h).transpose(2, 0, 3, 1)
    s_fin = s_fin.reshape(n_vg, n, vrep, bsz, h).transpose(3, 4, 0, 2, 1).reshape(bsz, h, n, n)
    return y, s_fin


def _rwkv7(c, wkv0, shift0, mu, w0, w2, a0, a2, g2, k_k, k_a, r_k, ln_g, ln_b):
    f32 = jnp.float32
    bsz, t, _ = c.shape
    prev = jnp.concatenate([shift0.astype(c.dtype), c[:, :-1]], axis=1)
    cm = c + mu * (prev - c)
    r, k, v, wl, al, gl = jnp.split(cm, RWKV_SPLITS, axis=-1)
    log_w = -jax.nn.softplus(-(w0 + jnp.tanh(wl) @ w2).astype(f32)) - 0.5
    decay = jnp.exp(-jnp.exp(log_w))
    a = jax.nn.sigmoid((a0 + al @ a2).astype(f32))
    g = jax.nn.sigmoid(gl) @ g2

    def heads(z):
        return z.astype(f32).reshape(bsz, t, RWKV_HEADS, RWKV_HEAD_DIM)

    kk = heads(k * k_k)
    kk = kk * lax.rsqrt(jnp.sum(kk * kk, axis=-1, keepdims=True) + 1e-12)
    k = k.astype(f32) * (1.0 + (a - 1.0) * k_a.astype(f32))
    rh, kh, vh, wh, ah = heads(r), heads(k), heads(v), heads(decay), heads(a)

    y, s_fin = _wkv_scan(rh, wh, kh, vh, kk, kk * ah, wkv0.astype(f32))
    mean = jnp.mean(y, axis=-1, keepdims=True)
    var = jnp.mean(jnp.square(y - mean), axis=-1, keepdims=True)
    y = ((y - mean) * lax.rsqrt(var + RWKV_GN_EPS)).reshape(bsz, t, RWKV_WIDTH) * ln_g.astype(f32) + ln_b.astype(f32)
    bonus = (jnp.sum(rh * kh * r_k.astype(f32), axis=-1, keepdims=True) * vh).reshape(bsz, t, RWKV_WIDTH)
    out = ((y + bonus) * g.astype(f32)).astype(c.dtype)
    return out, s_fin.astype(c.dtype), c[:, t - 1:]


def _pad_cols(w, n):
    return jnp.pad(w, ((0, 0), (0, 0), (0, n - w.shape[2])))


def _split_w_in(w_in):
    s = (0,) + IN_SPLITS + (D_IN,)
    seg = [w_in[:, :, s[i]:s[i + 1]] for i in range(6)]
    seg[3] = _pad_cols(seg[3], LANES)
    seg[4] = _pad_cols(seg[4], RW_PAD)
    return [z.astype(BF16) for z in seg]


RW_PAD = 2 * 14 * LANES
IN_TILES = (1024, 1024, N_KV_COLS, LANES, RW_PAD // 2, 1024)


def _layer(x, h, l, P, st, nsa_fn, fuse_ffn):
    lru_h0, lru_conv0, wkv0, shift0, ffn_conv0 = st
    bsz, t, _ = x.shape
    m = bsz * t
    norms = P['norms']
    xa, q, kv, nsa_g, rw, mg = [_mm_ws(h, w, l, tn) for w, tn in zip(P['w_in_seg'], IN_TILES)]
    xa = xa.reshape(bsz, t, LRU_WIDTH)
    nsa_g = nsa_g[:, :N_NSA_GATES].reshape(bsz, t, N_NSA_GATES)
    rw = rw[:, :RWKV_COLS].reshape(bsz, t, RWKV_COLS)
    o_a, lru_h, lru_conv = _rg_lru(xa, lru_h0, lru_conv0, P['lru_conv_w'][l], P['lru_conv_b'][l], P['lru_gate_w'][l], P['lru_gate_b'][l], P['lru_lambda'][l])
    q = q.reshape(bsz, t, NSA_HEADS, HEAD_DIM)
    kv = kv.reshape(bsz, t, 6, NSA_KV_HEADS, HEAD_DIM)
    o_b, nsa_rows, win_rows = nsa_fn(l, q, kv, nsa_g)
    o_c, wkv, shift = _rwkv7(rw, wkv0, shift0, P['rwkv_mu'][l], P['rwkv_w0'][l], P['rwkv_w2'][l], P['rwkv_a0'][l], P['rwkv_a2'][l], P['rwkv_g2'][l], P['rwkv_k_k'][l], P['rwkv_k_a'][l], P['rwkv_r_k'][l], P['rwkv_ln_g'][l], P['rwkv_ln_b'][l])

    def rows_bf(z):
        return z.reshape(m, BRANCH_WIDTH).astype(BF16)

    merged = _branch_merge(rows_bf(o_a), rows_bf(o_b), rows_bf(o_c), P['w_branch'], l, mg)
    x2, h2 = _mm_norm_res(merged, P['w_out_bf'], l, x.reshape(m, D_MODEL), norms[l, 1][None], norms[l, 2][None])
    if fuse_ffn:
        conv_w = P['ffn_conv_w'][l].reshape(FFN_CONV, 2, D_FF)
        conv_b = P['ffn_conv_b'][l].reshape(2, D_FF)
        act, ffn_conv = _ffn_up_act(h2, P['w_up'], l, conv_w, conv_b,
                                    ffn_conv0.reshape(bsz, FFN_CONV - 1, 2, D_FF), t)
        ffn_conv = ffn_conv.reshape(bsz, FFN_CONV - 1, 2 * D_FF)
    else:
        u = _mm_ws(h2, P['w_up'], l, 1024).reshape(bsz, t, 2 * D_FF)
        u, ffn_conv = _causal_conv(u, ffn_conv0, P['ffn_conv_w'][l], P['ffn_conv_b'][l])
        u_gate, u_val = jnp.split(u, 2, axis=-1)
        act = (jax.nn.gelu(u_gate) * u_val).reshape(m, D_FF).astype(BF16)
    g_next = norms[min(l + 1, DEPTH - 1), 0][None]
    x3, h_next = _mm_norm_res(act, P['w_down_bf'], l, x2, norms[l, 3][None], g_next)
    return x3.reshape(bsz, t, D_MODEL), h_next, (nsa_rows, win_rows, lru_h, lru_conv, wkv, shift, ffn_conv)


def kernel(x_prompt, x_sample, cache_nsa, cache_win, state_lru_h, state_lru_conv, state_rwkv_wkv, state_rwkv_shift, state_ffn_conv, page_table, norms, w_in, lru_conv_w, lru_conv_b, lru_gate_w, lru_gate_b, lru_lambda, nsa_phi, nsa_phi_b, rwkv_mu, rwkv_w0, rwkv_w2, rwkv_a0, rwkv_a2, rwkv_g2, rwkv_k_k, rwkv_k_a, rwkv_r_k, rwkv_ln_g, rwkv_ln_b, w_branch, w_out, w_up, ffn_conv_w, ffn_conv_b, w_down):
    P = {'norms': norms, 'w_in': w_in, 'lru_conv_w': lru_conv_w, 'lru_conv_b': lru_conv_b, 'lru_gate_w': lru_gate_w, 'lru_gate_b': lru_gate_b, 'lru_lambda': lru_lambda, 'rwkv_mu': rwkv_mu, 'rwkv_w0': rwkv_w0, 'rwkv_w2': rwkv_w2, 'rwkv_a0': rwkv_a0, 'rwkv_a2': rwkv_a2, 'rwkv_g2': rwkv_g2, 'rwkv_k_k': rwkv_k_k, 'rwkv_k_a': rwkv_k_a, 'rwkv_r_k': rwkv_r_k, 'rwkv_ln_g': rwkv_ln_g, 'rwkv_ln_b': rwkv_ln_b, 'w_branch': w_branch, 'w_out': w_out, 'w_up': w_up, 'ffn_conv_w': ffn_conv_w, 'ffn_conv_b': ffn_conv_b, 'w_down': w_down}

    def nsa_prompt_fn(l, q, kv, g):
        return _nsa_prompt(q, kv, g, nsa_phi[l], nsa_phi_b[l])

    def nsa_sample_fn(l, q, kv, g):
        return _nsa_sample(q, kv, g, nsa_phi[l], nsa_phi_b[l], cache_nsa, l, page_table, cache_win[l])

    P['w_in_seg'] = _split_w_in(w_in)
    P['w_out_bf'] = w_out.astype(BF16)
    P['w_down_bf'] = w_down.astype(BF16)

    def first_norm(x):
        return _rmsnorm(x, norms[0, 0]).reshape(-1, D_MODEL).astype(BF16)

    bsz, dt = x_prompt.shape[0], x_prompt.dtype
    zero_state = (jnp.zeros((bsz, LRU_WIDTH), dt), jnp.zeros((bsz, LRU_CONV - 1, LRU_WIDTH), dt), jnp.zeros((bsz, RWKV_HEADS, RWKV_HEAD_DIM, RWKV_HEAD_DIM), dt), jnp.zeros((bsz, 1, RWKV_COLS), dt), jnp.zeros((bsz, FFN_CONV - 1, 2 * D_FF), dt))
    y_p, y_s = x_prompt, x_sample
    h_p, h_s = first_norm(x_prompt), first_norm(x_sample)
    new_p, new_s = [], []
    for l in range(DEPTH):
        y_p, h_p, st_p = _layer(y_p, h_p, l, P, zero_state, nsa_prompt_fn, True)
        y_s, h_s, st_s = _layer(y_s, h_s, l, P, (state_lru_h[l], state_lru_conv[l], state_rwkv_wkv[l], state_rwkv_shift[l], state_ffn_conv[l]), nsa_sample_fn, False)
        new_p.append(st_p)
        new_s.append(st_s)

    def stacked(rows, i):
        return jnp.stack([r[i] for r in rows])

    return (y_p, y_s) + tuple(stacked(new_p, i) for i in range(7)) + tuple(stacked(new_s, i) for i in range(7))
```

```python
import functools

import jax
import jax.numpy as jnp
from jax import lax
from jax.experimental import pallas as pl
from jax.experimental.pallas import tpu as pltpu

D_MODEL = 2048
DEPTH = 4
PAST_LEN = 16384
PAGE_SIZE = 128
NORM_EPS = 1e-6
BRANCH_WIDTH = 1024
LRU_WIDTH = BRANCH_WIDTH
LRU_BLOCKS = 16
LRU_BLOCK_DIM = LRU_WIDTH // LRU_BLOCKS
LRU_CONV = 4
LRU_C = 8.0
NSA_HEADS = 16
NSA_KV_HEADS = 4
HEAD_DIM = 64
NSA_WIDTH = NSA_HEADS * HEAD_DIM
CMP_STRIDE = 16
CMP_BLOCK = 2 * CMP_STRIDE
SLC_BLOCK = 64
N_SELECT = 16
WINDOW = 512
Q_BLOCK = 128
ROPE_THETA = 10000.0
FORCE_BONUS = 1000.0
NEG_INF = -1e30
RWKV_HEADS = 16
RWKV_HEAD_DIM = 64
RWKV_WIDTH = RWKV_HEADS * RWKV_HEAD_DIM
DECAY_LORA = 64
ICL_LORA = 64
GATE_LORA = 160
RWKV_GN_EPS = 64e-5
D_FF = 3 * D_MODEL
FFN_CONV = 3
N_KV_COLS = 6 * NSA_KV_HEADS * HEAD_DIM
N_NSA_GATES = 3 * NSA_HEADS
RWKV_COLS = 3 * RWKV_WIDTH + DECAY_LORA + ICL_LORA + GATE_LORA
IN_SPLITS = (LRU_WIDTH, LRU_WIDTH + NSA_WIDTH, LRU_WIDTH + NSA_WIDTH + N_KV_COLS,
             LRU_WIDTH + NSA_WIDTH + N_KV_COLS + N_NSA_GATES,
             LRU_WIDTH + NSA_WIDTH + N_KV_COLS + N_NSA_GATES + RWKV_COLS)
D_IN = IN_SPLITS[-1] + 3 * D_MODEL
RWKV_SPLITS = (RWKV_WIDTH, 2 * RWKV_WIDTH, 3 * RWKV_WIDTH, 3 * RWKV_WIDTH + DECAY_LORA,
               3 * RWKV_WIDTH + DECAY_LORA + ICL_LORA)


def _mm_kernel(x_ref, w_ref, o_ref):
    @pl.when(pl.program_id(2) == 0)
    def _():
        o_ref[...] = jnp.zeros_like(o_ref)

    o_ref[...] += jnp.dot(x_ref[...].astype(jnp.bfloat16), w_ref[...].astype(jnp.bfloat16),
                          preferred_element_type=jnp.float32)


def _mm(x, w, tm=512, tn=512, tk=1024):
    m, k = x.shape
    n = w.shape[1]
    tm = min(tm, m)
    tk = min(tk, k)
    assert m % tm == 0 and k % tk == 0
    return pl.pallas_call(
        _mm_kernel,
        grid=(m // tm, pl.cdiv(n, tn), k // tk),
        in_specs=[pl.BlockSpec((tm, tk), lambda i, j, kk: (i, kk)),
                  pl.BlockSpec((tk, tn), lambda i, j, kk: (kk, j))],
        out_specs=pl.BlockSpec((tm, tn), lambda i, j, kk: (i, j)),
        out_shape=jax.ShapeDtypeStruct((m, n), jnp.float32),
        compiler_params=pltpu.CompilerParams(
            dimension_semantics=("parallel", "parallel", "arbitrary")),
        name="mm",
    )(x, w)


def _mm3(x, w):
    b, t, k = x.shape
    return _mm(x.reshape(b * t, k), w).reshape(b, t, w.shape[1])


VMEM_LIMIT = 56 * 1024 * 1024
BF16 = jnp.bfloat16


def _params(*sem):
    return pltpu.CompilerParams(dimension_semantics=sem, vmem_limit_bytes=VMEM_LIMIT)


def _mm_ws(x, w, l, tn, tm=1024):
    m, k = x.shape
    n = w.shape[2]
    tm = min(tm, m)
    cast = w.dtype != BF16

    def kern(x_ref, w_ref, o_ref, *scr):
        if cast:
            @pl.when(pl.program_id(1) == 0)
            def _():
                scr[0][...] = w_ref[...].astype(BF16)
            wv = scr[0][...]
        else:
            wv = w_ref[...]
        o_ref[...] = jnp.dot(x_ref[...], wv, preferred_element_type=jnp.float32)

    return pl.pallas_call(
        kern,
        grid=(n // tn, m // tm),
        in_specs=[pl.BlockSpec((tm, k), lambda j, i: (i, 0)),
                  pl.BlockSpec((None, k, tn), lambda j, i: (l, 0, j))],
        out_specs=pl.BlockSpec((tm, tn), lambda j, i: (i, j)),
        out_shape=jax.ShapeDtypeStruct((m, n), jnp.float32),
        scratch_shapes=[pltpu.VMEM((k, tn), BF16)] if cast else [],
        compiler_params=_params("arbitrary", "arbitrary"),
        name="mm_ws",
    )(x, w)


def _branch_merge(oa, ob, oc, wb, l, mg, tn=512, tm=1024):
    m, kb = oa.shape
    n = wb.shape[3]
    tm = min(tm, m)
    nj = n // tn

    def kern(a_ref, b_ref, c_ref, wa_ref, wb_ref, wc_ref, ga_ref, gb_ref, gc_ref, o_ref, sa, sb, sc):
        @pl.when(pl.program_id(1) == 0)
        def _():
            sa[...] = wa_ref[...].astype(BF16)
            sb[...] = wb_ref[...].astype(BF16)
            sc[...] = wc_ref[...].astype(BF16)

        def term(x_ref, s_ref, g_ref):
            return jax.nn.sigmoid(g_ref[...]) * jnp.dot(x_ref[...], s_ref[...], preferred_element_type=jnp.float32)

        o_ref[...] = (term(a_ref, sa, ga_ref) + term(b_ref, sb, gb_ref) + term(c_ref, sc, gc_ref)).astype(BF16)

    xspec = pl.BlockSpec((tm, kb), lambda j, i: (i, 0))

    def wspec(br):
        return pl.BlockSpec((None, None, kb, tn), lambda j, i: (l, br, 0, j))

    def gspec(br):
        return pl.BlockSpec((tm, tn), lambda j, i: (i, br * nj + j))

    return pl.pallas_call(
        kern,
        grid=(nj, m // tm),
        in_specs=[xspec, xspec, xspec, wspec(0), wspec(1), wspec(2), gspec(0), gspec(1), gspec(2)],
        out_specs=pl.BlockSpec((tm, tn), lambda j, i: (i, j)),
        out_shape=jax.ShapeDtypeStruct((m, n), BF16),
        scratch_shapes=[pltpu.VMEM((kb, tn), BF16)] * 3,
        compiler_params=_params("arbitrary", "arbitrary"),
        name="branch_merge",
    )(oa, ob, oc, wb, wb, wb, mg, mg, mg)


def _mm_norm_res(x, w, l, resid, g_post, g_next, tm=512, tk=1024):
    m, k = x.shape
    n = w.shape[2]
    tm = min(tm, m)
    nk = k // tk

    def kern(x_ref, w_ref, r_ref, gp_ref, gn_ref, xo_ref, ho_ref, acc):
        kk = pl.program_id(1)

        @pl.when(kk == 0)
        def _():
            acc[...] = jnp.zeros_like(acc)

        acc[...] += jnp.dot(x_ref[...], w_ref[...], preferred_element_type=jnp.float32)

        @pl.when(kk == nk - 1)
        def _():
            z = acc[...]
            y = z * lax.rsqrt(jnp.mean(z * z, axis=-1, keepdims=True) + NORM_EPS) * gp_ref[...]
            xn = r_ref[...] + y
            xo_ref[...] = xn
            hn = xn * lax.rsqrt(jnp.mean(xn * xn, axis=-1, keepdims=True) + NORM_EPS) * gn_ref[...]
            ho_ref[...] = hn.astype(BF16)

    row = pl.BlockSpec((tm, n), lambda i, kk: (i, 0))
    gain = pl.BlockSpec((1, n), lambda i, kk: (0, 0))
    return pl.pallas_call(
        kern,
        grid=(m // tm, nk),
        in_specs=[pl.BlockSpec((tm, tk), lambda i, kk: (i, kk)),
                  pl.BlockSpec((None, tk, n), lambda i, kk: (l, kk, 0)),
                  row, gain, gain],
        out_specs=[row, row],
        out_shape=[jax.ShapeDtypeStruct((m, n), jnp.float32), jax.ShapeDtypeStruct((m, n), BF16)],
        scratch_shapes=[pltpu.VMEM((tm, n), jnp.float32)],
        compiler_params=_params("arbitrary", "arbitrary"),
        name="mm_norm_res",
    )(x, w, resid, g_post, g_next)


def _ffn_up_act(h, w_up, l, conv_w, conv_b, conv0, t_len, tm=512, tn=512):
    m, k = h.shape
    f = conv_w.shape[2]
    bsz = m // t_len
    nj = f // tn
    tpb = t_len // tm
    taps = FFN_CONV - 1

    def kern(h_ref, wg_ref, wv_ref, cw_ref, cb_ref, c0_ref, act_ref, st_ref, sg, sv, pg, pv):
        i = pl.program_id(1)

        @pl.when(i == 0)
        def _():
            sg[...] = wg_ref[...].astype(BF16)
            sv[...] = wv_ref[...].astype(BF16)

        @pl.when(i % tpb == 0)
        def _():
            pg[...] = c0_ref[:, 0, :]
            pv[...] = c0_ref[:, 1, :]

        rid = lax.broadcasted_iota(jnp.int32, (tm, tn), 0)

        def conv(u, prev, half):
            u1 = jnp.where(rid == 0, prev[1:2], pltpu.roll(u, 1, axis=0))
            u2 = jnp.where(rid == 0, prev[0:1], jnp.where(rid == 1, prev[1:2], pltpu.roll(u, 2, axis=0)))
            return (cb_ref[half:half + 1, :] + u2 * cw_ref[0, half:half + 1, :]
                    + u1 * cw_ref[1, half:half + 1, :] + u * cw_ref[2, half:half + 1, :])

        ug = jnp.dot(h_ref[...], sg[...], preferred_element_type=jnp.float32)
        uv = jnp.dot(h_ref[...], sv[...], preferred_element_type=jnp.float32)
        act_ref[...] = (jax.nn.gelu(conv(ug, pg[...], 0)) * conv(uv, pv[...], 1)).astype(BF16)
        pg[...] = ug[tm - taps:]
        pv[...] = uv[tm - taps:]

        @pl.when(i % tpb == tpb - 1)
        def _():
            st_ref[:, 0, :] = ug[tm - taps:]
            st_ref[:, 1, :] = uv[tm - taps:]

    def wspec(off):
        return pl.BlockSpec((None, k, tn), lambda j, i: (l, 0, j + off))

    stspec = pl.BlockSpec((None, taps, 2, tn), lambda j, i: (i // tpb, 0, 0, j))
    return pl.pallas_call(
        kern,
        grid=(nj, m // tm),
        in_specs=[pl.BlockSpec((tm, k), lambda j, i: (i, 0)), wspec(0), wspec(nj),
                  pl.BlockSpec((FFN_CONV, 2, tn), lambda j, i: (0, 0, j)),
                  pl.BlockSpec((2, tn), lambda j, i: (0, j)), stspec],
        out_specs=[pl.BlockSpec((tm, tn), lambda j, i: (i, j)), stspec],
        out_shape=[jax.ShapeDtypeStruct((m, f), BF16), jax.ShapeDtypeStruct((bsz, taps, 2, f), jnp.float32)],
        scratch_shapes=[pltpu.VMEM((k, tn), BF16), pltpu.VMEM((k, tn), BF16),
                        pltpu.VMEM((taps, tn), jnp.float32), pltpu.VMEM((taps, tn), jnp.float32)],
        compiler_params=_params("arbitrary", "arbitrary"),
        name="ffn_up_act",
    )(h, w_up, w_up, conv_w, conv_b, conv0)


PAGES_PER_STEP = 8


def _past_compress(pool, l, page_table, w2):
    bsz, n_pages = page_table.shape
    cpp = PAGE_SIZE // CMP_STRIDE
    n_steps = n_pages // PAGES_PER_STEP
    width = 2 * NSA_KV_HEADS * HEAD_DIM
    pairs = width // (2 * HEAD_DIM)
    rows = PAGES_PER_STEP * cpp

    def kern(pt_ref, *refs):
        pages = refs[:PAGES_PER_STEP]
        w_ref, o_ref = refs[PAGES_PER_STEP], refs[PAGES_PER_STEP + 1]
        x = jnp.concatenate([p[...] for p in pages], axis=0).reshape(rows, CMP_STRIDE, width)
        acc = [jnp.zeros((rows, 4 * HEAD_DIM), jnp.float32) for _ in range(pairs)]
        for j in range(CMP_STRIDE):
            xj = x[:, j, :].astype(BF16)
            for pr in range(pairs):
                kind = pr // (pairs // 2)
                acc[pr] = acc[pr] + jnp.dot(xj[:, pr * 2 * HEAD_DIM:(pr + 1) * 2 * HEAD_DIM], w_ref[kind, j],
                                            preferred_element_type=jnp.float32)
        for pr in range(pairs):
            o_ref[:, pr * 4 * HEAD_DIM:(pr + 1) * 4 * HEAD_DIM] = acc[pr]

    def page_spec(jj):
        return pl.BlockSpec((None, None, PAGE_SIZE, width),
                            lambda b, s, pt: (l, pt[b, s * PAGES_PER_STEP + jj], 0, 0))

    return pl.pallas_call(
        kern,
        grid_spec=pltpu.PrefetchScalarGridSpec(
            num_scalar_prefetch=1,
            grid=(bsz, n_steps),
            in_specs=[page_spec(jj) for jj in range(PAGES_PER_STEP)]
            + [pl.BlockSpec(w2.shape, lambda b, s, pt: (0, 0, 0, 0))],
            out_specs=pl.BlockSpec((None, rows, 2 * width), lambda b, s, pt: (b, s, 0)),
        ),
        out_shape=jax.ShapeDtypeStruct((bsz, n_pages * cpp, 2 * width), jnp.float32),
        compiler_params=_params("arbitrary", "arbitrary"),
        name="past_compress",
    )(page_table, *([pool] * PAGES_PER_STEP), w2)


def _rmsnorm(x, g):
    xf = x.astype(jnp.float32)
    y = xf * lax.rsqrt(jnp.mean(xf * xf, axis=-1, keepdims=True) + NORM_EPS)
    return (y * g.astype(jnp.float32)).astype(x.dtype)


def _causal_conv(x, buf, w, b):
    k, t = w.shape[0], x.shape[1]
    xp = jnp.concatenate([buf.astype(x.dtype), x], axis=1)
    y = b + xp[:, 0:t] * w[0]
    for j in range(1, k):
        y = y + xp[:, j:j + t] * w[j]
    return y, xp[:, xp.shape[1] - (k - 1):]


def _rope(x, pos):
    half = x.shape[-1] // 2
    inv = ROPE_THETA ** (-jnp.arange(half, dtype=jnp.float32) / half)
    ang = pos.astype(jnp.float32)[:, None] * inv[None, :]
    cos = jnp.cos(ang)[None, :, None, :].astype(x.dtype)
    sin = jnp.sin(ang)[None, :, None, :].astype(x.dtype)
    x1, x2 = x[..., :half], x[..., half:]
    return jnp.concatenate([x1 * cos - x2 * sin, x2 * cos + x1 * sin], axis=-1)


def _masked_softmax(s, mask):
    p = jax.nn.softmax(jnp.where(mask, s, NEG_INF), axis=-1)
    return jnp.where(mask, p, 0.0)


def _lin_combine(left, right):
    a1, b1 = left
    a2, b2 = right
    return a1 * a2, a2 * b1 + b2


def _pad_rows(z, mult):
    pad = -z.shape[1] % mult
    return jnp.pad(z, ((0, 0), (0, pad)) + ((0, 0),) * (z.ndim - 2))


def _rg_lru(xa, h0, conv_buf, conv_w, conv_b, gate_w, gate_b, lam):
    f32 = jnp.float32
    bsz, t, _ = xa.shape
    xc, new_buf = _causal_conv(xa, conv_buf, conv_w, conv_b)
    xb = xc.reshape(bsz, t, LRU_BLOCKS, LRU_BLOCK_DIM)
    gates = jnp.einsum('btnd,gnde->gbtne', xb, gate_w).reshape(2, bsz, t, LRU_WIDTH)
    gates = gates.astype(f32) + gate_b.astype(f32)[:, None, None, :]
    r, i = jax.nn.sigmoid(gates[0]), jax.nn.sigmoid(gates[1])
    log_a = -LRU_C * r * jax.nn.softplus(-lam.astype(f32))
    a = jnp.exp(log_a)
    b = jnp.sqrt(-jnp.expm1(2.0 * log_a)) * (i * xc.astype(f32))
    b = b.at[:, 0].add(a[:, 0] * h0.astype(f32))
    _, h = lax.associative_scan(_lin_combine, (a, b), axis=1)
    return h.astype(xa.dtype), h[:, -1].astype(xa.dtype), new_buf


def _compress(z, w, b):
    bsz, length, g, d = z.shape
    ch = z.reshape(bsz, length // CMP_STRIDE, CMP_STRIDE, g, d)
    head = jnp.einsum('bcjgd,jde->bcge', ch, w[:CMP_STRIDE])
    tail = jnp.einsum('bcjgd,jde->bcge', ch, w[CMP_STRIDE:])
    return head[:, :-1] + tail[:, 1:] + b


def _nsa_core(q, q_rot, qpos, kc, vc, selected, kw, vw, kwpos, gates):
    f32 = jnp.float32
    bsz, t = q.shape[0], q.shape[1]
    hpg = NSA_HEADS // NSA_KV_HEADS
    scale = HEAD_DIM ** -0.5
    qg = q.reshape(bsz, t, NSA_KV_HEADS, hpg, HEAD_DIM)
    qrg = q_rot.reshape(bsz, t, NSA_KV_HEADS, hpg, HEAD_DIM)
    n_cmp = kc.shape[1]
    cmp_end = jnp.arange(n_cmp) * CMP_STRIDE + (CMP_BLOCK - 1)
    m_c = (cmp_end[None, :] <= qpos[:, None])[None, :, None, None, :]
    p_c = _masked_softmax(jnp.einsum('btghd,bngd->btghn', qg, kc).astype(f32) * scale, m_c)
    o_c = jnp.einsum('btghn,bngd->btghd', p_c.astype(vc.dtype), vc)
    per = SLC_BLOCK // CMP_STRIDE
    n_slc = (n_cmp + 1) // per
    imp = jnp.pad(p_c.sum(axis=3), ((0, 0), (0, 0), (0, 0), (0, 1))).reshape(bsz, t, NSA_KV_HEADS, n_slc, per)
    imp = imp.sum(-1) + jnp.pad(imp[..., :-1, per - 1], ((0, 0), (0, 0), (0, 0), (1, 0)))
    blk = jnp.arange(n_slc)[None, :]
    qblk = (qpos // SLC_BLOCK)[:, None]
    valid = blk * SLC_BLOCK <= qpos[:, None]
    forced = (blk == 0) | (blk == qblk) | (blk == qblk - 1)
    score = jnp.where(valid[None, :, None, :], imp + FORCE_BONUS * forced[None, :, None, :], NEG_INF)
    n_top = min(N_SELECT, n_slc)
    _, idx = lax.top_k(score, n_top)
    o_s = selected(qrg, idx)
    dpos = qpos[:, None] - kwpos[None, :]
    m_w = ((dpos >= 0) & (dpos < WINDOW) & (kwpos[None, :] >= 0))[None, :, None, None, :]
    p_w = _masked_softmax(jnp.einsum('btghd,bkgd->btghk', qrg, kw).astype(f32) * scale, m_w)
    o_w = jnp.einsum('btghk,bkgd->btghd', p_w.astype(vw.dtype), vw)
    gt = jax.nn.sigmoid(gates.astype(f32)).astype(q.dtype).reshape(bsz, t, NSA_KV_HEADS, hpg, 3, 1)
    o = gt[..., 0, :] * o_c + gt[..., 1, :] * o_s + gt[..., 2, :] * o_w
    return o.reshape(bsz, t, NSA_WIDTH)


HPG = NSA_HEADS // NSA_KV_HEADS
CMP_PER_SLC = SLC_BLOCK // CMP_STRIDE
SEL_CHUNK = 512
WIN_KEYS = WINDOW + Q_BLOCK
DROPPED = -3e38


def _softmax_rows(s, ok):
    sm = jnp.where(ok[None], s, NEG_INF)
    e = jnp.exp(sm - jnp.max(sm, axis=-1, keepdims=True))
    p = e / jnp.sum(e, axis=-1, keepdims=True)
    return jnp.where(ok[None], p, 0.0)


def _nsa_prompt_kernel(q_ref, qr_ref, g_ref, kct_ref, vc_ref, kst_ref, vs_ref, kwt_ref, vw_ref,
                       impt_ref, exp_ref, o_ref, *, n_cmp):
    f32, bf = jnp.float32, jnp.bfloat16
    i = pl.program_id(2)
    start = i * Q_BLOCK
    rows = HPG * Q_BLOCK
    n_cpad = kct_ref.shape[-1]
    n_slc = impt_ref.shape[0]
    scale = HEAD_DIM ** -0.5
    q = (q_ref[0, 0, 0] * scale).astype(bf)
    qr = (qr_ref[0, 0, 0] * scale).astype(bf)
    t_col = start + lax.broadcasted_iota(jnp.int32, (Q_BLOCK, 1), 0)

    s = jnp.dot(q, kct_ref[0, 0], preferred_element_type=f32).reshape(HPG, Q_BLOCK, n_cpad)
    n_io = lax.broadcasted_iota(jnp.int32, (Q_BLOCK, n_cpad), 1)
    ok_c = jnp.where(n_io < n_cmp, n_io * CMP_STRIDE + (CMP_BLOCK - 1), 2 ** 30) <= t_col
    p = _softmax_rows(s, ok_c)
    o_c = jnp.dot(p.reshape(rows, n_cpad).astype(bf), vc_ref[0, 0], preferred_element_type=f32)

    psum = p[0] + p[1] + p[2] + p[3]
    hi = psum.astype(bf)
    r1 = psum - hi.astype(f32)
    mid = r1.astype(bf)
    lo = (r1 - mid.astype(f32)).astype(bf)
    nt = (((1,), (1,)), ((), ()))
    imp_t = (lax.dot_general(impt_ref[...], hi, nt, preferred_element_type=f32)
             + lax.dot_general(impt_ref[...], mid, nt, preferred_element_type=f32)
             + lax.dot_general(impt_ref[...], lo, nt, preferred_element_type=f32))
    j_io = lax.broadcasted_iota(jnp.int32, (n_slc, Q_BLOCK), 0)
    t_row = start + lax.broadcasted_iota(jnp.int32, (n_slc, Q_BLOCK), 1)
    qblk = t_row // SLC_BLOCK
    forced = jnp.where(j_io == 0, 1.0, 0.0) + jnp.where(j_io == qblk, 1.0, 0.0) + jnp.where(j_io == qblk - 1, 1.0, 0.0)
    forced = jnp.minimum(forced, 1.0)
    score = jnp.where(j_io * SLC_BLOCK <= t_row, imp_t + FORCE_BONUS * forced, NEG_INF)

    def pick(_, carry):
        sc, sel = carry
        best = jnp.max(sc, axis=0, keepdims=True)
        first = jnp.min(jnp.where(sc == best, j_io, n_slc), axis=0, keepdims=True)
        hit = j_io == first
        return jnp.where(hit, DROPPED, sc), jnp.where(hit, 1.0, sel)

    _, sel_t = lax.fori_loop(0, min(N_SELECT, n_slc), pick, (score, jnp.zeros((n_slc, Q_BLOCK), f32)))
    sel = sel_t.T.astype(bf)

    def chunk(c, carry):
        m, l, acc = carry
        off = pl.multiple_of(c * SEL_CHUNK, SEL_CHUNK)
        kt = kst_ref[0, 0, :, pl.ds(off, SEL_CHUNK)]
        v = vs_ref[0, 0, pl.ds(off, SEL_CHUNK), :]
        sc = jnp.dot(qr, kt, preferred_element_type=f32).reshape(HPG, Q_BLOCK, SEL_CHUNK)
        chosen = jnp.dot(sel, exp_ref[:, pl.ds(off, SEL_CHUNK)], preferred_element_type=f32)
        kpos = off + lax.broadcasted_iota(jnp.int32, (Q_BLOCK, SEL_CHUNK), 1)
        ok = jnp.where(kpos <= t_col, chosen, 0.0) > 0.5
        sc = jnp.where(ok[None], sc, NEG_INF)
        m_new = jnp.maximum(m, jnp.max(sc, axis=-1, keepdims=True))
        alpha = jnp.exp(m - m_new)
        pe = jnp.exp(sc - m_new)
        l = alpha * l + jnp.sum(pe, axis=-1, keepdims=True)
        pv = jnp.dot(pe.reshape(rows, SEL_CHUNK).astype(bf), v, preferred_element_type=f32)
        return m_new, l, alpha * acc + pv.reshape(HPG, Q_BLOCK, HEAD_DIM)

    n_chunks = (start + Q_BLOCK + SEL_CHUNK - 1) // SEL_CHUNK
    init = (jnp.full((HPG, Q_BLOCK, 1), NEG_INF, f32), jnp.zeros((HPG, Q_BLOCK, 1), f32),
            jnp.zeros((HPG, Q_BLOCK, HEAD_DIM), f32))
    _, l_s, acc_s = lax.fori_loop(0, n_chunks, chunk, init)
    o_s = (acc_s / l_s).reshape(rows, HEAD_DIM)

    n_keys = kwt_ref.shape[-1]
    wk = min(WIN_KEYS, n_keys)
    k0 = pl.multiple_of(jnp.maximum(start + Q_BLOCK - wk, 0), Q_BLOCK)
    kt = kwt_ref[0, 0, :, pl.ds(k0, wk)]
    v = vw_ref[0, 0, pl.ds(k0, wk), :]
    sw = jnp.dot(qr, kt, preferred_element_type=f32).reshape(HPG, Q_BLOCK, wk)
    dpos = t_col - (k0 + lax.broadcasted_iota(jnp.int32, (Q_BLOCK, wk), 1))
    ok_w = jnp.where(dpos >= 0, dpos, WINDOW) < WINDOW
    pw = _softmax_rows(sw, ok_w)
    o_w = jnp.dot(pw.reshape(rows, wk).astype(bf), v, preferred_element_type=f32)

    gt = jax.nn.sigmoid(g_ref[0, 0])
    heads = []
    for h in range(HPG):
        r = slice(h * Q_BLOCK, (h + 1) * Q_BLOCK)
        heads.append(gt[:, 3 * h:3 * h + 1] * o_c[r] + gt[:, 3 * h + 1:3 * h + 2] * o_s[r]
                     + gt[:, 3 * h + 2:3 * h + 3] * o_w[r])
    o_ref[0] = jnp.concatenate(heads, axis=1).astype(o_ref.dtype)


def _nsa_prep_kernel(q_ref, kv_ref, cos_ref, sin_ref, qb_ref, qrb_ref, kvr_ref, kst_ref, kwt_ref, vs_ref, vw_ref):
    cos, sin = cos_ref[...], sin_ref[...]
    lane = lax.broadcasted_iota(jnp.int32, (Q_BLOCK, LANES), 1)
    first_half = (lane % HEAD_DIM) < HEAD_DIM // 2

    def rope(x):
        partner = jnp.where(first_half, pltpu.roll(x, LANES - HEAD_DIM // 2, axis=1),
                            pltpu.roll(x, HEAD_DIM // 2, axis=1))
        return x * cos + partner * sin

    for c in range(NSA_WIDTH // LANES):
        x = q_ref[:, c * LANES:(c + 1) * LANES]
        xr = rope(x)
        for e in range(LANES // HEAD_DIM):
            g, h = divmod(c * (LANES // HEAD_DIM) + e, HPG)
            rows = slice(h * Q_BLOCK, (h + 1) * Q_BLOCK)
            qb_ref[g, rows, :] = x[:, e * HEAD_DIM:(e + 1) * HEAD_DIM].astype(BF16)
            qrb_ref[g, rows, :] = xr[:, e * HEAD_DIM:(e + 1) * HEAD_DIM].astype(BF16)

    kind_w = NSA_KV_HEADS * HEAD_DIM
    for kind in range(6):
        for c in range(kind_w // LANES):
            col = kind * kind_w + c * LANES
            x = kv_ref[:, col:col + LANES]
            if kind in (2, 4):
                x = rope(x)
                t_ref = kst_ref if kind == 2 else kwt_ref
                t_ref[c * LANES:(c + 1) * LANES, :] = x.T.astype(BF16)
            if kind in (3, 5):
                v_ref = vs_ref if kind == 3 else vw_ref
                for e in range(LANES // HEAD_DIM):
                    v_ref[c * (LANES // HEAD_DIM) + e] = x[:, e * HEAD_DIM:(e + 1) * HEAD_DIM].astype(BF16)
            kvr_ref[:, col:col + LANES] = x


def _rope_tables(pos):
    half = HEAD_DIM // 2
    inv = ROPE_THETA ** (-jnp.arange(half, dtype=jnp.float32) / half)
    ang = pos.astype(jnp.float32)[:, None] * inv[None, :]
    cos, sin = jnp.cos(ang), jnp.sin(ang)
    reps = LANES // HEAD_DIM
    return jnp.tile(jnp.concatenate([cos, cos], axis=1), (1, reps)), jnp.tile(jnp.concatenate([-sin, sin], axis=1), (1, reps))


def _nsa_prompt(q, kv, gates, phi, phi_b, bsz, s):
    g, d = NSA_KV_HEADS, HEAD_DIM
    n_qb = s // Q_BLOCK
    rows = HPG * Q_BLOCK
    kind_w = g * d
    cos, sin = _rope_tables(jnp.arange(s))
    tile = lambda w: pl.BlockSpec((Q_BLOCK, w), lambda b, i: (b * n_qb + i, 0))
    tab = pl.BlockSpec((Q_BLOCK, LANES), lambda b, i: (i, 0))
    qb_spec = pl.BlockSpec((None, g, None, rows, d), lambda b, i: (b, 0, i, 0, 0))
    kt_spec = pl.BlockSpec((None, kind_w, Q_BLOCK), lambda b, i: (b, 0, i))
    v_spec = pl.BlockSpec((None, g, Q_BLOCK, d), lambda b, i: (b, 0, i, 0))
    qb_shape = jax.ShapeDtypeStruct((bsz, g, n_qb, rows, d), BF16)
    kt_shape = jax.ShapeDtypeStruct((bsz, kind_w, s), BF16)
    v_shape = jax.ShapeDtypeStruct((bsz, g, s, d), BF16)
    qb, qrb, kv_rot, kst, kwt, vs, vw = pl.pallas_call(
        _nsa_prep_kernel,
        grid=(bsz, n_qb),
        in_specs=[tile(NSA_WIDTH), tile(N_KV_COLS), tab, tab],
        out_specs=[qb_spec, qb_spec, tile(N_KV_COLS), kt_spec, kt_spec, v_spec, v_spec],
        out_shape=[qb_shape, qb_shape, jax.ShapeDtypeStruct(kv.shape, jnp.float32), kt_shape, kt_shape, v_shape, v_shape],
        compiler_params=_params("arbitrary", "arbitrary"),
        name="nsa_prep",
    )(q, kv, cos, sin)

    pages = jnp.arange(bsz * n_qb, dtype=jnp.int32).reshape(bsz, n_qb)
    ht = _past_compress(kv.reshape(1, bsz * n_qb, Q_BLOCK, N_KV_COLS), 0, pages, _pair_weights(phi))
    ht = ht.reshape(bsz, ht.shape[1], 2, g, 2, d)
    n_cmp = ht.shape[1] - 1
    n_cpad = -(-(n_cmp + 1) // LANES) * LANES
    n_slc = (n_cmp + 1) // CMP_PER_SLC
    pad_c = ((0, 0), (0, n_cpad - n_cmp), (0, 0), (0, 0))
    kc = jnp.pad(ht[:, :-1, 0, :, 0] + ht[:, 1:, 0, :, 1] + phi_b[0], pad_c)
    vc = jnp.pad(ht[:, :-1, 1, :, 0] + ht[:, 1:, 1, :, 1] + phi_b[1], pad_c)
    kct = kc.transpose(0, 2, 3, 1).astype(BF16)
    vcg = vc.transpose(0, 2, 1, 3).astype(BF16)
    gates_g = gates[:, :N_NSA_GATES].reshape(bsz, s, g, 3 * HPG).transpose(0, 2, 1, 3)

    n_io = jnp.arange(n_cpad)[None, :]
    j_io = jnp.arange(n_slc)[:, None]
    imp_t = ((n_io >= CMP_PER_SLC * j_io - 1) & (n_io < CMP_PER_SLC * (j_io + 1))).astype(BF16)
    expand = (jnp.arange(s)[None, :] // SLC_BLOCK == j_io).astype(BF16)
    qspec = pl.BlockSpec((1, 1, 1, rows, d), lambda b, gg, i: (b, gg, i, 0, 0))

    def whole(shape):
        return pl.BlockSpec((1, 1) + shape, lambda b, gg, i: (b, gg, 0, 0))

    o = pl.pallas_call(
        functools.partial(_nsa_prompt_kernel, n_cmp=n_cmp),
        grid=(bsz, g, n_qb),
        in_specs=[qspec, qspec,
                  pl.BlockSpec((1, 1, Q_BLOCK, 3 * HPG), lambda b, gg, i: (b, gg, i, 0)),
                  whole((d, n_cpad)), whole((n_cpad, d)),
                  whole((d, s)), whole((s, d)), whole((d, s)), whole((s, d)),
                  pl.BlockSpec((n_slc, n_cpad), lambda b, gg, i: (0, 0)),
                  pl.BlockSpec((n_slc, s), lambda b, gg, i: (0, 0))],
        out_specs=pl.BlockSpec((1, Q_BLOCK, HPG * d), lambda b, gg, i: (b, i, gg)),
        out_shape=jax.ShapeDtypeStruct((bsz, s, NSA_WIDTH), BF16),
        compiler_params=_params("arbitrary", "arbitrary", "arbitrary"),
        name="nsa_prompt",
    )(qb, qrb, gates_g, kct, vcg, kst.reshape(bsz, g, d, s), vs, kwt.reshape(bsz, g, d, s), vw, imp_t, expand)

    n_win = min(WINDOW, s)
    kv_rot = kv_rot.reshape(bsz, s, 6, g, d)
    return o.reshape(bsz * s, NSA_WIDTH), kv_rot[:, :, :4], kv_rot[:, s - n_win:, 4:]


def _pair_weights(phi):
    d = HEAD_DIM
    ht = jnp.concatenate([phi[:, :CMP_STRIDE], phi[:, CMP_STRIDE:]], axis=-1)
    z = jnp.zeros_like(ht)
    return jnp.concatenate([jnp.concatenate([ht, z], axis=-1), jnp.concatenate([z, ht], axis=-1)], axis=-2).astype(BF16)


def _sample_compressed(cache, l, page_table, new_rows, phi, phi_b):
    bsz = page_table.shape[0]
    g, d = NSA_KV_HEADS, HEAD_DIM
    pool = cache.reshape(cache.shape[0], cache.shape[1], PAGE_SIZE, 4 * g * d)
    ht = _past_compress(pool, l, page_table, _pair_weights(phi))
    ht = ht.reshape(bsz, ht.shape[1], 2, g, 2, d)
    out = []
    for kind in range(2):
        ch = _pad_rows(new_rows[kind], SLC_BLOCK)
        ch = ch.reshape(bsz, ch.shape[1] // CMP_STRIDE, CMP_STRIDE, g, d)
        head = jnp.concatenate([ht[:, :, kind, :, 0], jnp.einsum('bcjgd,jde->bcge', ch, phi[kind, :CMP_STRIDE])], axis=1)
        tail = jnp.concatenate([ht[:, :, kind, :, 1], jnp.einsum('bcjgd,jde->bcge', ch, phi[kind, CMP_STRIDE:])], axis=1)
        out.append(head[:, :-1] + tail[:, 1:] + phi_b[kind])
    return out


def _nsa_sample(q, kv, gates, phi, phi_b, cache, l, page_table, win_buf):
    bsz, t = q.shape[0], q.shape[1]
    pos = PAST_LEN + jnp.arange(t)
    k_cmp, v_cmp, k_slc, v_slc, k_win, v_win = [kv[:, :, i] for i in range(6)]
    q_rot, k_slc, k_win = _rope(q, pos), _rope(k_slc, pos), _rope(k_win, pos)
    kc, vc = _sample_compressed(cache, l, page_table, (k_cmp, v_cmp), phi, phi_b)
    kw = jnp.concatenate([win_buf[:, :, 0].astype(q.dtype), k_win], axis=1)
    vw = jnp.concatenate([win_buf[:, :, 1].astype(q.dtype), v_win], axis=1)
    n_win = win_buf.shape[1]
    kwpos = PAST_LEN - n_win + jnp.arange(n_win + t)

    def selected(q_rot_g, idx):
        return _sample_selected(q_rot_g, idx, pos, cache, l, page_table, k_slc, v_slc)

    o = _nsa_core(q, q_rot, pos, kc, vc, selected, kw, vw, kwpos, gates)
    rows = jnp.stack([k_cmp, v_cmp, k_slc, v_slc], axis=2)
    win_rows = jnp.stack([k_win, v_win], axis=2)
    return o, rows, win_rows


def _sample_selected_kernel(pt_ref, *refs):
    pages = refs[:PAGES_PER_STEP]
    q_ref, ok_ref, new_ref, oknew_ref, o_ref, m_s, l_s, acc_s = refs[PAGES_PER_STEP:]
    step = pl.program_id(1)
    kw = NSA_KV_HEADS * HEAD_DIM

    @pl.when(step == 0)
    def _():
        m_s[...] = jnp.full_like(m_s, NEG_INF)
        l_s[...] = jnp.zeros_like(l_s)
        acc_s[...] = jnp.zeros_like(acc_s)

    def update(kv, ok):
        k = kv[:, :kw].astype(BF16)
        v = kv[:, kw:].astype(BF16)
        sc = lax.dot_general(q_ref[...], k, (((1,), (1,)), ((), ())), preferred_element_type=jnp.float32)
        sc = jnp.where(ok > 0.5, sc, NEG_INF)
        m_new = jnp.maximum(m_s[...], jnp.max(sc, axis=-1, keepdims=True))
        alpha = jnp.exp(m_s[...] - m_new)
        p = jnp.exp(sc - m_new)
        l_s[...] = alpha * l_s[...] + jnp.sum(p, axis=-1, keepdims=True)
        acc_s[...] = alpha * acc_s[...] + jnp.dot(p.astype(BF16), v, preferred_element_type=jnp.float32)
        m_s[...] = m_new

    update(jnp.concatenate([p[...] for p in pages], axis=0), ok_ref[...])

    @pl.when(step == pl.num_programs(1) - 1)
    def _():
        update(new_ref[...], oknew_ref[...])
        o_ref[...] = acc_s[...] / l_s[...]


def _sample_selected(q_rot_g, idx, qpos, cache, l, page_table, k_new, v_new):
    bsz, t = q_rot_g.shape[0], q_rot_g.shape[1]
    g, d = NSA_KV_HEADS, HEAD_DIM
    n_pages = page_table.shape[1]
    past = n_pages * PAGE_SIZE
    rows = g * t * HPG
    kw = g * d
    n_steps = n_pages // PAGES_PER_STEP
    keys_per_step = PAGES_PER_STEP * PAGE_SIZE
    pool = cache.reshape(cache.shape[0], cache.shape[1], PAGE_SIZE, 4 * kw)

    qg = (q_rot_g * HEAD_DIM ** -0.5).transpose(0, 2, 1, 3, 4).reshape(bsz, g, t * HPG, d)
    q_bd = (qg[:, :, :, None, :] * jnp.eye(g, dtype=qg.dtype)[None, :, None, :, None]).reshape(bsz, rows, kw).astype(BF16)

    n_blk = past // SLC_BLOCK
    chosen = (idx[..., None] == jnp.arange(n_blk + 1)).any(axis=-2)

    def by_row(z):
        z = jnp.broadcast_to(z.transpose(0, 2, 1, 3)[:, :, :, None, :], (bsz, g, t, HPG, z.shape[-1]))
        return z.reshape(bsz, rows, z.shape[-1]).astype(jnp.float32)

    ok_past = by_row(jnp.repeat(chosen[..., :n_blk], SLC_BLOCK, axis=-1))
    j = jnp.arange(PAGE_SIZE)
    new_ok = (j[None, :] < t) & (past + j[None, :] <= qpos[:, None])
    ok_new = by_row(chosen[..., n_blk:] & new_ok[None, :, None, :])
    new_kv = jnp.concatenate([k_new.reshape(bsz, t, kw), v_new.reshape(bsz, t, kw)], axis=-1)
    new_kv = jnp.pad(new_kv, ((0, 0), (0, PAGE_SIZE - t), (0, 0)))

    def page_spec(jj):
        return pl.BlockSpec((None, None, PAGE_SIZE, 2 * kw),
                            lambda b, s, pt: (l, pt[b, s * PAGES_PER_STEP + jj], 0, 1))

    o = pl.pallas_call(
        _sample_selected_kernel,
        grid_spec=pltpu.PrefetchScalarGridSpec(
            num_scalar_prefetch=1,
            grid=(bsz, n_steps),
            in_specs=[page_spec(jj) for jj in range(PAGES_PER_STEP)]
            + [pl.BlockSpec((None, rows, kw), lambda b, s, pt: (b, 0, 0)),
               pl.BlockSpec((None, rows, keys_per_step), lambda b, s, pt: (b, 0, s)),
               pl.BlockSpec((None, PAGE_SIZE, 2 * kw), lambda b, s, pt: (b, 0, 0)),
               pl.BlockSpec((None, rows, PAGE_SIZE), lambda b, s, pt: (b, 0, 0))],
            out_specs=pl.BlockSpec((None, rows, kw), lambda b, s, pt: (b, 0, 0)),
            scratch_shapes=[pltpu.VMEM((rows, 1), jnp.float32), pltpu.VMEM((rows, 1), jnp.float32),
                            pltpu.VMEM((rows, kw), jnp.float32)],
        ),
        out_shape=jax.ShapeDtypeStruct((bsz, rows, kw), jnp.float32),
        compiler_params=_params("arbitrary", "arbitrary"),
        name="sample_selected",
    )(page_table, *([pool] * PAGES_PER_STEP), q_bd, ok_past, new_kv, ok_new)
    o = o.reshape(bsz, g, t, HPG, g, d)
    o = jnp.stack([o[:, gg, :, :, gg] for gg in range(g)], axis=1)
    return o.transpose(0, 2, 1, 3, 4)


LANES = 128
WKV_T_CHUNK = 64


def _wkv_kernel(w_ref, kk_ref, kka_ref, k_ref, r_ref, v_ref, s0_ref, y_ref, sfin_ref, s_scr):
    c = pl.program_id(0)
    n_vg = s_scr.shape[0]

    @pl.when(c == 0)
    def _():
        s_scr[...] = s0_ref[...]

    def step(t, carry):
        w, kk, kka, k, r = w_ref[t], kk_ref[t], kka_ref[t], k_ref[t], r_ref[t]
        for vg in range(n_vg):
            s = s_scr[vg]
            sa = jnp.sum(s * kk, axis=0, keepdims=True)
            s = s * w - kka * sa + k * v_ref[t, vg:vg + 1, :]
            s_scr[vg] = s
            y_ref[t, vg:vg + 1, :] = jnp.sum(s * r, axis=0, keepdims=True)
        return carry

    lax.fori_loop(0, w_ref.shape[0], step, 0)

    @pl.when(c == pl.num_programs(0) - 1)
    def _():
        sfin_ref[...] = s_scr[...]


def _wkv_scan(r, w, k, v, kk, kka, s0):
    bsz, t, h, n = r.shape
    bh = bsz * h
    vrep = LANES // bh
    n_vg = n // vrep
    tc = min(WKV_T_CHUNK, t)

    def key_tiles(z):
        z = z.transpose(1, 3, 0, 2).reshape(t, n, 1, bh)
        return jnp.broadcast_to(z, (t, n, vrep, bh)).reshape(t, n, LANES)

    v_rows = v.transpose(1, 3, 0, 2).reshape(t, n_vg, LANES)
    s_tiles = s0.transpose(2, 3, 0, 1).reshape(n_vg, vrep, n, bh).transpose(0, 2, 1, 3).reshape(n_vg, n, LANES)
    kspec = pl.BlockSpec((tc, n, LANES), lambda c: (c, 0, 0))
    vspec = pl.BlockSpec((tc, n_vg, LANES), lambda c: (c, 0, 0))
    sspec = pl.BlockSpec((n_vg, n, LANES), lambda c: (0, 0, 0))
    y, s_fin = pl.pallas_call(
        _wkv_kernel,
        grid=(t // tc,),
        in_specs=[kspec] * 5 + [vspec, sspec],
        out_specs=[vspec, sspec],
        out_shape=[jax.ShapeDtypeStruct((t, n_vg, LANES), jnp.float32),
                   jax.ShapeDtypeStruct((n_vg, n, LANES), jnp.float32)],
        scratch_shapes=[pltpu.VMEM((n_vg, n, LANES), jnp.float32)],
        compiler_params=pltpu.CompilerParams(dimension_semantics=("arbitrary",),
                                             vmem_limit_bytes=48 * 1024 * 1024),
        name="wkv_scan",
    )(key_tiles(w), key_tiles(kk), key_tiles(kka), key_tiles(k), key_tiles(r), v_rows, s_tiles)
    y = y.reshape(t, n, bsz, h).transpose(2, 0, 3, 1)
    s_fin = s_fin.reshape(n_vg, n, vrep, bsz, h).transpose(3, 4, 0, 2, 1).reshape(bsz, h, n, n)
    return y, s_fin


def _rwkv7(c, wkv0, shift0, mu, w0, w2, a0, a2, g2, k_k, k_a, r_k, ln_g, ln_b):
    f32 = jnp.float32
    bsz, t, _ = c.shape
    prev = jnp.concatenate([shift0.astype(c.dtype), c[:, :-1]], axis=1)
    cm = c + mu * (prev - c)
    r, k, v, wl, al, gl = jnp.split(cm, RWKV_SPLITS, axis=-1)
    log_w = -jax.nn.softplus(-(w0 + jnp.tanh(wl) @ w2).astype(f32)) - 0.5
    decay = jnp.exp(-jnp.exp(log_w))
    a = jax.nn.sigmoid((a0 + al @ a2).astype(f32))
    g = jax.nn.sigmoid(gl) @ g2

    def heads(z):
        return z.astype(f32).reshape(bsz, t, RWKV_HEADS, RWKV_HEAD_DIM)

    kk = heads(k * k_k)
    kk = kk * lax.rsqrt(jnp.sum(kk * kk, axis=-1, keepdims=True) + 1e-12)
    k = k.astype(f32) * (1.0 + (a - 1.0) * k_a.astype(f32))
    rh, kh, vh, wh, ah = heads(r), heads(k), heads(v), heads(decay), heads(a)

    y, s_fin = _wkv_scan(rh, wh, kh, vh, kk, kk * ah, wkv0.astype(f32))
    mean = jnp.mean(y, axis=-1, keepdims=True)
    var = jnp.mean(jnp.square(y - mean), axis=-1, keepdims=True)
    y = ((y - mean) * lax.rsqrt(var + RWKV_GN_EPS)).reshape(bsz, t, RWKV_WIDTH) * ln_g.astype(f32) + ln_b.astype(f32)
    bonus = (jnp.sum(rh * kh * r_k.astype(f32), axis=-1, keepdims=True) * vh).reshape(bsz, t, RWKV_WIDTH)
    out = ((y + bonus) * g.astype(f32)).astype(c.dtype)
    return out, s_fin.astype(c.dtype), c[:, t - 1:]


def _pad_cols(w, n):
    return jnp.pad(w, ((0, 0), (0, 0), (0, n - w.shape[2])))


def _split_w_in(w_in):
    s = (0,) + IN_SPLITS + (D_IN,)
    seg = [w_in[:, :, s[i]:s[i + 1]] for i in range(6)]
    seg[3] = _pad_cols(seg[3], LANES)
    seg[4] = _pad_cols(seg[4], RW_PAD)
    return [z.astype(BF16) for z in seg]


RW_PAD = 2 * 14 * LANES
IN_TILES = (1024, 1024, N_KV_COLS, LANES, RW_PAD // 2, 1024)


def _layer(x, h, l, P, st, nsa_fn, fuse_ffn):
    lru_h0, lru_conv0, wkv0, shift0, ffn_conv0 = st
    bsz, t, _ = x.shape
    m = bsz * t
    norms = P['norms']
    xa, q, kv, nsa_g, rw, mg = [_mm_ws(h, w, l, tn) for w, tn in zip(P['w_in_seg'], IN_TILES)]
    xa = xa.reshape(bsz, t, LRU_WIDTH)
    rw = rw[:, :RWKV_COLS].reshape(bsz, t, RWKV_COLS)
    o_a, lru_h, lru_conv = _rg_lru(xa, lru_h0, lru_conv0, P['lru_conv_w'][l], P['lru_conv_b'][l], P['lru_gate_w'][l], P['lru_gate_b'][l], P['lru_lambda'][l])
    o_b, nsa_rows, win_rows = nsa_fn(l, q, kv, nsa_g, bsz, t)
    o_c, wkv, shift = _rwkv7(rw, wkv0, shift0, P['rwkv_mu'][l], P['rwkv_w0'][l], P['rwkv_w2'][l], P['rwkv_a0'][l], P['rwkv_a2'][l], P['rwkv_g2'][l], P['rwkv_k_k'][l], P['rwkv_k_a'][l], P['rwkv_r_k'][l], P['rwkv_ln_g'][l], P['rwkv_ln_b'][l])

    def rows_bf(z):
        return z.reshape(m, BRANCH_WIDTH).astype(BF16)

    merged = _branch_merge(rows_bf(o_a), rows_bf(o_b), rows_bf(o_c), P['w_branch'], l, mg)
    x2, h2 = _mm_norm_res(merged, P['w_out_bf'], l, x.reshape(m, D_MODEL), norms[l, 1][None], norms[l, 2][None])
    if fuse_ffn:
        conv_w = P['ffn_conv_w'][l].reshape(FFN_CONV, 2, D_FF)
        conv_b = P['ffn_conv_b'][l].reshape(2, D_FF)
        act, ffn_conv = _ffn_up_act(h2, P['w_up'], l, conv_w, conv_b,
                                    ffn_conv0.reshape(bsz, FFN_CONV - 1, 2, D_FF), t)
        ffn_conv = ffn_conv.reshape(bsz, FFN_CONV - 1, 2 * D_FF)
    else:
        u = _mm_ws(h2, P['w_up'], l, 1024).reshape(bsz, t, 2 * D_FF)
        u, ffn_conv = _causal_conv(u, ffn_conv0, P['ffn_conv_w'][l], P['ffn_conv_b'][l])
        u_gate, u_val = jnp.split(u, 2, axis=-1)
        act = (jax.nn.gelu(u_gate) * u_val).reshape(m, D_FF).astype(BF16)
    g_next = norms[min(l + 1, DEPTH - 1), 0][None]
    x3, h_next = _mm_norm_res(act, P['w_down_bf'], l, x2, norms[l, 3][None], g_next)
    return x3.reshape(bsz, t, D_MODEL), h_next, (nsa_rows, win_rows, lru_h, lru_conv, wkv, shift, ffn_conv)


def kernel(x_prompt, x_sample, cache_nsa, cache_win, state_lru_h, state_lru_conv, state_rwkv_wkv, state_rwkv_shift, state_ffn_conv, page_table, norms, w_in, lru_conv_w, lru_conv_b, lru_gate_w, lru_gate_b, lru_lambda, nsa_phi, nsa_phi_b, rwkv_mu, rwkv_w0, rwkv_w2, rwkv_a0, rwkv_a2, rwkv_g2, rwkv_k_k, rwkv_k_a, rwkv_r_k, rwkv_ln_g, rwkv_ln_b, w_branch, w_out, w_up, ffn_conv_w, ffn_conv_b, w_down):
    P = {'norms': norms, 'w_in': w_in, 'lru_conv_w': lru_conv_w, 'lru_conv_b': lru_conv_b, 'lru_gate_w': lru_gate_w, 'lru_gate_b': lru_gate_b, 'lru_lambda': lru_lambda, 'rwkv_mu': rwkv_mu, 'rwkv_w0': rwkv_w0, 'rwkv_w2': rwkv_w2, 'rwkv_a0': rwkv_a0, 'rwkv_a2': rwkv_a2, 'rwkv_g2': rwkv_g2, 'rwkv_k_k': rwkv_k_k, 'rwkv_k_a': rwkv_k_a, 'rwkv_r_k': rwkv_r_k, 'rwkv_ln_g': rwkv_ln_g, 'rwkv_ln_b': rwkv_ln_b, 'w_branch': w_branch, 'w_out': w_out, 'w_up': w_up, 'ffn_conv_w': ffn_conv_w, 'ffn_conv_b': ffn_conv_b, 'w_down': w_down}

    def nsa_prompt_fn(l, q, kv, g, bsz, t):
        return _nsa_prompt(q, kv, g, nsa_phi[l], nsa_phi_b[l], bsz, t)

    def nsa_sample_fn(l, q, kv, g, bsz, t):
        q = q.reshape(bsz, t, NSA_HEADS, HEAD_DIM)
        kv = kv.reshape(bsz, t, 6, NSA_KV_HEADS, HEAD_DIM)
        g = g[:, :N_NSA_GATES].reshape(bsz, t, N_NSA_GATES)
        return _nsa_sample(q, kv, g, nsa_phi[l], nsa_phi_b[l], cache_nsa, l, page_table, cache_win[l])

    P['w_in_seg'] = _split_w_in(w_in)
    P['w_out_bf'] = w_out.astype(BF16)
    P['w_down_bf'] = w_down.astype(BF16)

    def first_norm(x):
        return _rmsnorm(x, norms[0, 0]).reshape(-1, D_MODEL).astype(BF16)

    bsz, dt = x_prompt.shape[0], x_prompt.dtype
    zero_state = (jnp.zeros((bsz, LRU_WIDTH), dt), jnp.zeros((bsz, LRU_CONV - 1, LRU_WIDTH), dt), jnp.zeros((bsz, RWKV_HEADS, RWKV_HEAD_DIM, RWKV_HEAD_DIM), dt), jnp.zeros((bsz, 1, RWKV_COLS), dt), jnp.zeros((bsz, FFN_CONV - 1, 2 * D_FF), dt))
    y_p, y_s = x_prompt, x_sample
    h_p, h_s = first_norm(x_prompt), first_norm(x_sample)
    new_p, new_s = [], []
    for l in range(DEPTH):
        y_p, h_p, st_p = _layer(y_p, h_p, l, P, zero_state, nsa_prompt_fn, True)
        y_s, h_s, st_s = _layer(y_s, h_s, l, P, (state_lru_h[l], state_lru_conv[l], state_rwkv_wkv[l], state_rwkv_shift[l], state_ffn_conv[l]), nsa_sample_fn, False)
        new_p.append(st_p)
        new_s.append(st_s)

    def stacked(rows, i):
        return jnp.stack([r[i] for r in rows])

    return (y_p, y_s) + tuple(stacked(new_p, i) for i in range(7)) + tuple(stacked(new_s, i) for i in range(7))
```

```python
import functools

import jax
import jax.numpy as jnp
from jax import lax
from jax.experimental import pallas as pl
from jax.experimental.pallas import tpu as pltpu

D_MODEL = 2048
DEPTH = 4
PAST_LEN = 16384
PAGE_SIZE = 128
NORM_EPS = 1e-6
BRANCH_WIDTH = 1024
LRU_WIDTH = BRANCH_WIDTH
LRU_BLOCKS = 16
LRU_BLOCK_DIM = LRU_WIDTH // LRU_BLOCKS
LRU_CONV = 4
LRU_C = 8.0
NSA_HEADS = 16
NSA_KV_HEADS = 4
HEAD_DIM = 64
NSA_WIDTH = NSA_HEADS * HEAD_DIM
CMP_STRIDE = 16
CMP_BLOCK = 2 * CMP_STRIDE
SLC_BLOCK = 64
N_SELECT = 16
WINDOW = 512
Q_BLOCK = 128
ROPE_THETA = 10000.0
FORCE_BONUS = 1000.0
NEG_INF = -1e30
RWKV_HEADS = 16
RWKV_HEAD_DIM = 64
RWKV_WIDTH = RWKV_HEADS * RWKV_HEAD_DIM
DECAY_LORA = 64
ICL_LORA = 64
GATE_LORA = 160
RWKV_GN_EPS = 64e-5
D_FF = 3 * D_MODEL
FFN_CONV = 3
N_KV_COLS = 6 * NSA_KV_HEADS * HEAD_DIM
N_NSA_GATES = 3 * NSA_HEADS
RWKV_COLS = 3 * RWKV_WIDTH + DECAY_LORA + ICL_LORA + GATE_LORA
IN_SPLITS = (LRU_WIDTH, LRU_WIDTH + NSA_WIDTH, LRU_WIDTH + NSA_WIDTH + N_KV_COLS,
             LRU_WIDTH + NSA_WIDTH + N_KV_COLS + N_NSA_GATES,
             LRU_WIDTH + NSA_WIDTH + N_KV_COLS + N_NSA_GATES + RWKV_COLS)
D_IN = IN_SPLITS[-1] + 3 * D_MODEL
RWKV_SPLITS = (RWKV_WIDTH, 2 * RWKV_WIDTH, 3 * RWKV_WIDTH, 3 * RWKV_WIDTH + DECAY_LORA,
               3 * RWKV_WIDTH + DECAY_LORA + ICL_LORA)


def _mm_kernel(x_ref, w_ref, o_ref):
    @pl.when(pl.program_id(2) == 0)
    def _():
        o_ref[...] = jnp.zeros_like(o_ref)

    o_ref[...] += jnp.dot(x_ref[...].astype(jnp.bfloat16), w_ref[...].astype(jnp.bfloat16),
                          preferred_element_type=jnp.float32)


def _mm(x, w, tm=512, tn=512, tk=1024):
    m, k = x.shape
    n = w.shape[1]
    tm = min(tm, m)
    tk = min(tk, k)
    assert m % tm == 0 and k % tk == 0
    return pl.pallas_call(
        _mm_kernel,
        grid=(m // tm, pl.cdiv(n, tn), k // tk),
        in_specs=[pl.BlockSpec((tm, tk), lambda i, j, kk: (i, kk)),
                  pl.BlockSpec((tk, tn), lambda i, j, kk: (kk, j))],
        out_specs=pl.BlockSpec((tm, tn), lambda i, j, kk: (i, j)),
        out_shape=jax.ShapeDtypeStruct((m, n), jnp.float32),
        compiler_params=pltpu.CompilerParams(
            dimension_semantics=("parallel", "parallel", "arbitrary")),
        name="mm",
    )(x, w)


def _mm3(x, w):
    b, t, k = x.shape
    return _mm(x.reshape(b * t, k), w).reshape(b, t, w.shape[1])


VMEM_LIMIT = 56 * 1024 * 1024
BF16 = jnp.bfloat16


def _params(*sem):
    return pltpu.CompilerParams(dimension_semantics=sem, vmem_limit_bytes=VMEM_LIMIT)


def _mm_ws(x, w, l, tn, tm=1024):
    m, k = x.shape
    n = w.shape[2]
    tm = min(tm, m)
    cast = w.dtype != BF16

    def kern(x_ref, w_ref, o_ref, *scr):
        if cast:
            @pl.when(pl.program_id(1) == 0)
            def _():
                scr[0][...] = w_ref[...].astype(BF16)
            wv = scr[0][...]
        else:
            wv = w_ref[...]
        o_ref[...] = jnp.dot(x_ref[...], wv, preferred_element_type=jnp.float32)

    return pl.pallas_call(
        kern,
        grid=(n // tn, m // tm),
        in_specs=[pl.BlockSpec((tm, k), lambda j, i: (i, 0)),
                  pl.BlockSpec((None, k, tn), lambda j, i: (l, 0, j))],
        out_specs=pl.BlockSpec((tm, tn), lambda j, i: (i, j)),
        out_shape=jax.ShapeDtypeStruct((m, n), jnp.float32),
        scratch_shapes=[pltpu.VMEM((k, tn), BF16)] if cast else [],
        compiler_params=_params("arbitrary", "arbitrary"),
        name="mm_ws",
    )(x, w)


def _branch_merge(oa, ob, oc, wb, l, mg, tn=512, tm=1024):
    m, kb = oa.shape
    n = wb.shape[3]
    tm = min(tm, m)
    nj = n // tn

    def kern(a_ref, b_ref, c_ref, wa_ref, wb_ref, wc_ref, ga_ref, gb_ref, gc_ref, o_ref, sa, sb, sc):
        @pl.when(pl.program_id(1) == 0)
        def _():
            sa[...] = wa_ref[...].astype(BF16)
            sb[...] = wb_ref[...].astype(BF16)
            sc[...] = wc_ref[...].astype(BF16)

        def term(x_ref, s_ref, g_ref):
            return jax.nn.sigmoid(g_ref[...]) * jnp.dot(x_ref[...], s_ref[...], preferred_element_type=jnp.float32)

        o_ref[...] = (term(a_ref, sa, ga_ref) + term(b_ref, sb, gb_ref) + term(c_ref, sc, gc_ref)).astype(BF16)

    xspec = pl.BlockSpec((tm, kb), lambda j, i: (i, 0))

    def wspec(br):
        return pl.BlockSpec((None, None, kb, tn), lambda j, i: (l, br, 0, j))

    def gspec(br):
        return pl.BlockSpec((tm, tn), lambda j, i: (i, br * nj + j))

    return pl.pallas_call(
        kern,
        grid=(nj, m // tm),
        in_specs=[xspec, xspec, xspec, wspec(0), wspec(1), wspec(2), gspec(0), gspec(1), gspec(2)],
        out_specs=pl.BlockSpec((tm, tn), lambda j, i: (i, j)),
        out_shape=jax.ShapeDtypeStruct((m, n), BF16),
        scratch_shapes=[pltpu.VMEM((kb, tn), BF16)] * 3,
        compiler_params=_params("arbitrary", "arbitrary"),
        name="branch_merge",
    )(oa, ob, oc, wb, wb, wb, mg, mg, mg)


def _mm_norm_res(x, w, l, resid, g_post, g_next, tm=512, tk=1024):
    m, k = x.shape
    n = w.shape[2]
    tm = min(tm, m)
    nk = k // tk

    def kern(x_ref, w_ref, r_ref, gp_ref, gn_ref, xo_ref, ho_ref, acc):
        kk = pl.program_id(1)

        @pl.when(kk == 0)
        def _():
            acc[...] = jnp.zeros_like(acc)

        acc[...] += jnp.dot(x_ref[...], w_ref[...], preferred_element_type=jnp.float32)

        @pl.when(kk == nk - 1)
        def _():
            z = acc[...]
            y = z * lax.rsqrt(jnp.mean(z * z, axis=-1, keepdims=True) + NORM_EPS) * gp_ref[...]
            xn = r_ref[...] + y
            xo_ref[...] = xn
            hn = xn * lax.rsqrt(jnp.mean(xn * xn, axis=-1, keepdims=True) + NORM_EPS) * gn_ref[...]
            ho_ref[...] = hn.astype(BF16)

    row = pl.BlockSpec((tm, n), lambda i, kk: (i, 0))
    gain = pl.BlockSpec((1, n), lambda i, kk: (0, 0))
    return pl.pallas_call(
        kern,
        grid=(m // tm, nk),
        in_specs=[pl.BlockSpec((tm, tk), lambda i, kk: (i, kk)),
                  pl.BlockSpec((None, tk, n), lambda i, kk: (l, kk, 0)),
                  row, gain, gain],
        out_specs=[row, row],
        out_shape=[jax.ShapeDtypeStruct((m, n), jnp.float32), jax.ShapeDtypeStruct((m, n), BF16)],
        scratch_shapes=[pltpu.VMEM((tm, n), jnp.float32)],
        compiler_params=_params("arbitrary", "arbitrary"),
        name="mm_norm_res",
    )(x, w, resid, g_post, g_next)


def _ffn_up_act(h, w_up, l, conv_w, conv_b, conv0, t_len, tm=512, tn=512):
    m, k = h.shape
    f = conv_w.shape[2]
    bsz = m // t_len
    nj = f // tn
    tpb = t_len // tm
    taps = FFN_CONV - 1

    def kern(h_ref, wg_ref, wv_ref, cw_ref, cb_ref, c0_ref, act_ref, st_ref, sg, sv, pg, pv):
        i = pl.program_id(1)

        @pl.when(i == 0)
        def _():
            sg[...] = wg_ref[...].astype(BF16)
            sv[...] = wv_ref[...].astype(BF16)

        @pl.when(i % tpb == 0)
        def _():
            pg[...] = c0_ref[:, 0, :]
            pv[...] = c0_ref[:, 1, :]

        rid = lax.broadcasted_iota(jnp.int32, (tm, tn), 0)

        def conv(u, prev, half):
            u1 = jnp.where(rid == 0, prev[1:2], pltpu.roll(u, 1, axis=0))
            u2 = jnp.where(rid == 0, prev[0:1], jnp.where(rid == 1, prev[1:2], pltpu.roll(u, 2, axis=0)))
            return (cb_ref[half:half + 1, :] + u2 * cw_ref[0, half:half + 1, :]
                    + u1 * cw_ref[1, half:half + 1, :] + u * cw_ref[2, half:half + 1, :])

        ug = jnp.dot(h_ref[...], sg[...], preferred_element_type=jnp.float32)
        uv = jnp.dot(h_ref[...], sv[...], preferred_element_type=jnp.float32)
        act_ref[...] = (jax.nn.gelu(conv(ug, pg[...], 0)) * conv(uv, pv[...], 1)).astype(BF16)
        pg[...] = ug[tm - taps:]
        pv[...] = uv[tm - taps:]

        @pl.when(i % tpb == tpb - 1)
        def _():
            st_ref[:, 0, :] = ug[tm - taps:]
            st_ref[:, 1, :] = uv[tm - taps:]

    def wspec(off):
        return pl.BlockSpec((None, k, tn), lambda j, i: (l, 0, j + off))

    stspec = pl.BlockSpec((None, taps, 2, tn), lambda j, i: (i // tpb, 0, 0, j))
    return pl.pallas_call(
        kern,
        grid=(nj, m // tm),
        in_specs=[pl.BlockSpec((tm, k), lambda j, i: (i, 0)), wspec(0), wspec(nj),
                  pl.BlockSpec((FFN_CONV, 2, tn), lambda j, i: (0, 0, j)),
                  pl.BlockSpec((2, tn), lambda j, i: (0, j)), stspec],
        out_specs=[pl.BlockSpec((tm, tn), lambda j, i: (i, j)), stspec],
        out_shape=[jax.ShapeDtypeStruct((m, f), BF16), jax.ShapeDtypeStruct((bsz, taps, 2, f), jnp.float32)],
        scratch_shapes=[pltpu.VMEM((k, tn), BF16), pltpu.VMEM((k, tn), BF16),
                        pltpu.VMEM((taps, tn), jnp.float32), pltpu.VMEM((taps, tn), jnp.float32)],
        compiler_params=_params("arbitrary", "arbitrary"),
        name="ffn_up_act",
    )(h, w_up, w_up, conv_w, conv_b, conv0)


PAGES_PER_STEP = 8


def _past_compress(pool, l, page_table, w2):
    bsz, n_pages = page_table.shape
    cpp = PAGE_SIZE // CMP_STRIDE
    n_steps = n_pages // PAGES_PER_STEP
    width = 2 * NSA_KV_HEADS * HEAD_DIM
    pairs = width // (2 * HEAD_DIM)
    rows = PAGES_PER_STEP * cpp

    def kern(pt_ref, *refs):
        pages = refs[:PAGES_PER_STEP]
        w_ref, o_ref = refs[PAGES_PER_STEP], refs[PAGES_PER_STEP + 1]
        x = jnp.concatenate([p[...] for p in pages], axis=0).reshape(rows, CMP_STRIDE, width)
        acc = [jnp.zeros((rows, 4 * HEAD_DIM), jnp.float32) for _ in range(pairs)]
        for j in range(CMP_STRIDE):
            xj = x[:, j, :].astype(BF16)
            for pr in range(pairs):
                kind = pr // (pairs // 2)
                acc[pr] = acc[pr] + jnp.dot(xj[:, pr * 2 * HEAD_DIM:(pr + 1) * 2 * HEAD_DIM], w_ref[kind, j],
                                            preferred_element_type=jnp.float32)
        for pr in range(pairs):
            o_ref[:, pr * 4 * HEAD_DIM:(pr + 1) * 4 * HEAD_DIM] = acc[pr]

    def page_spec(jj):
        return pl.BlockSpec((None, None, PAGE_SIZE, width),
                            lambda b, s, pt: (l, pt[b, s * PAGES_PER_STEP + jj], 0, 0))

    return pl.pallas_call(
        kern,
        grid_spec=pltpu.PrefetchScalarGridSpec(
            num_scalar_prefetch=1,
            grid=(bsz, n_steps),
            in_specs=[page_spec(jj) for jj in range(PAGES_PER_STEP)]
            + [pl.BlockSpec(w2.shape, lambda b, s, pt: (0, 0, 0, 0))],
            out_specs=pl.BlockSpec((None, rows, 2 * width), lambda b, s, pt: (b, s, 0)),
        ),
        out_shape=jax.ShapeDtypeStruct((bsz, n_pages * cpp, 2 * width), jnp.float32),
        compiler_params=_params("arbitrary", "arbitrary"),
        name="past_compress",
    )(page_table, *([pool] * PAGES_PER_STEP), w2)


def _rmsnorm(x, g):
    xf = x.astype(jnp.float32)
    y = xf * lax.rsqrt(jnp.mean(xf * xf, axis=-1, keepdims=True) + NORM_EPS)
    return (y * g.astype(jnp.float32)).astype(x.dtype)


def _causal_conv(x, buf, w, b):
    k, t = w.shape[0], x.shape[1]
    xp = jnp.concatenate([buf.astype(x.dtype), x], axis=1)
    y = b + xp[:, 0:t] * w[0]
    for j in range(1, k):
        y = y + xp[:, j:j + t] * w[j]
    return y, xp[:, xp.shape[1] - (k - 1):]


def _rope(x, pos):
    half = x.shape[-1] // 2
    inv = ROPE_THETA ** (-jnp.arange(half, dtype=jnp.float32) / half)
    ang = pos.astype(jnp.float32)[:, None] * inv[None, :]
    cos = jnp.cos(ang)[None, :, None, :].astype(x.dtype)
    sin = jnp.sin(ang)[None, :, None, :].astype(x.dtype)
    x1, x2 = x[..., :half], x[..., half:]
    return jnp.concatenate([x1 * cos - x2 * sin, x2 * cos + x1 * sin], axis=-1)


def _masked_softmax(s, mask):
    p = jax.nn.softmax(jnp.where(mask, s, NEG_INF), axis=-1)
    return jnp.where(mask, p, 0.0)


def _lin_combine(left, right):
    a1, b1 = left
    a2, b2 = right
    return a1 * a2, a2 * b1 + b2


def _pad_rows(z, mult):
    pad = -z.shape[1] % mult
    return jnp.pad(z, ((0, 0), (0, pad)) + ((0, 0),) * (z.ndim - 2))


def _rg_lru(xa, h0, conv_buf, conv_w, conv_b, gate_w, gate_b, lam):
    f32 = jnp.float32
    bsz, t, _ = xa.shape
    xc, new_buf = _causal_conv(xa, conv_buf, conv_w, conv_b)
    xb = xc.reshape(bsz, t, LRU_BLOCKS, LRU_BLOCK_DIM)
    gates = jnp.einsum('btnd,gnde->gbtne', xb, gate_w).reshape(2, bsz, t, LRU_WIDTH)
    gates = gates.astype(f32) + gate_b.astype(f32)[:, None, None, :]
    r, i = jax.nn.sigmoid(gates[0]), jax.nn.sigmoid(gates[1])
    log_a = -LRU_C * r * jax.nn.softplus(-lam.astype(f32))
    a = jnp.exp(log_a)
    b = jnp.sqrt(-jnp.expm1(2.0 * log_a)) * (i * xc.astype(f32))
    b = b.at[:, 0].add(a[:, 0] * h0.astype(f32))
    _, h = lax.associative_scan(_lin_combine, (a, b), axis=1)
    return h.astype(xa.dtype), h[:, -1].astype(xa.dtype), new_buf


def _lru_glue(xa, h0, conv_buf, conv_w, conv_b, gate_w, gate_b, lam, bsz, t):
    o, h_last, new_buf = _rg_lru(xa.reshape(bsz, t, LRU_WIDTH), h0, conv_buf, conv_w, conv_b, gate_w, gate_b, lam)
    return o.reshape(bsz * t, LRU_WIDTH), h_last, new_buf


SUBLANES = 8


def _lru_kernel(x_ref, cw_ref, cb_ref, gw_ref, gb_ref, sp_ref, h0_ref, c0_ref, o_ref, hl_ref,
                a_s, b_s, h_s, hc, prev, *, tpb):
    i = pl.program_id(0)
    tm, width = x_ref.shape
    taps = LRU_CONV - 1

    @pl.when(i % tpb == 0)
    def _():
        hc[...] = h0_ref[...]
        prev[...] = c0_ref[...]

    x = x_ref[...]
    rid = lax.broadcasted_iota(jnp.int32, (tm, width), 0)
    xc = cb_ref[...]
    for j in range(taps):
        back = taps - j
        u = pltpu.roll(x, back, axis=0)
        for r in range(back):
            u = jnp.where(rid == r, prev[taps - back + r:taps - back + r + 1, :], u)
        xc = xc + u * cw_ref[j:j + 1, :]
    xc = xc + x * cw_ref[taps:taps + 1, :]
    prev[...] = x[tm - taps:]

    gates = jnp.dot(xc.astype(BF16), gw_ref[...], preferred_element_type=jnp.float32) + gb_ref[...]
    r_g = jax.nn.sigmoid(gates[:, :width])
    i_g = jax.nn.sigmoid(gates[:, width:])
    log_a = -LRU_C * r_g * sp_ref[...]
    a = jnp.exp(log_a)
    a_s[...] = a
    b_s[...] = jnp.sqrt(-jnp.tanh(log_a) * (jnp.exp(2.0 * log_a) + 1.0)) * (i_g * xc)

    def group(gidx, h):
        base = pl.multiple_of(gidx * SUBLANES, SUBLANES)
        a8 = a_s[pl.ds(base, SUBLANES), :]
        b8 = b_s[pl.ds(base, SUBLANES), :]
        rows = []
        for r in range(SUBLANES):
            h = a8[r:r + 1] * h + b8[r:r + 1]
            rows.append(h)
        h_s[pl.ds(base, SUBLANES), :] = jnp.concatenate(rows, axis=0)
        return h

    h_last = lax.fori_loop(0, tm // SUBLANES, group, hc[...])
    hc[...] = h_last
    hl_ref[...] = h_last
    o_ref[...] = h_s[...].astype(o_ref.dtype)


def _lru_prompt(xa, h0, conv_buf, conv_w, conv_b, gate_w, gate_b, lam, bsz, t, tm=512):
    m, width = xa.shape
    tpb = t // tm
    taps = LRU_CONV - 1
    eye = jnp.eye(LRU_BLOCKS, dtype=gate_w.dtype)
    gw = jnp.einsum('gnde,nm->gndme', gate_w, eye).reshape(2, width, width)
    gw = jnp.concatenate([gw[0], gw[1]], axis=1).astype(BF16)
    gb = gate_b.reshape(1, 2 * width)
    sp = jax.nn.softplus(-lam.astype(jnp.float32)).reshape(1, width)
    row = lambda n: pl.BlockSpec((n, width), lambda i: (0, 0))
    o, h_last = pl.pallas_call(
        functools.partial(_lru_kernel, tpb=tpb),
        grid=(m // tm,),
        in_specs=[pl.BlockSpec((tm, width), lambda i: (i, 0)), row(LRU_CONV), row(1),
                  pl.BlockSpec((width, 2 * width), lambda i: (0, 0)),
                  pl.BlockSpec((1, 2 * width), lambda i: (0, 0)), row(1),
                  pl.BlockSpec((None, 1, width), lambda i: (i // tpb, 0, 0)),
                  pl.BlockSpec((None, taps, width), lambda i: (i // tpb, 0, 0))],
        out_specs=[pl.BlockSpec((tm, width), lambda i: (i, 0)),
                   pl.BlockSpec((None, 1, width), lambda i: (i // tpb, 0, 0))],
        out_shape=[jax.ShapeDtypeStruct((m, width), BF16), jax.ShapeDtypeStruct((bsz, 1, width), jnp.float32)],
        scratch_shapes=[pltpu.VMEM((tm, width), jnp.float32)] * 3
        + [pltpu.VMEM((1, width), jnp.float32), pltpu.VMEM((taps, width), jnp.float32)],
        compiler_params=_params("arbitrary"),
        name="rg_lru",
    )(xa, conv_w, conv_b.reshape(1, width), gw, gb, sp, h0.reshape(bsz, 1, width), conv_buf)
    new_buf = xa.reshape(bsz, t, width)[:, t - taps:]
    return o, h_last.reshape(bsz, width), new_buf


def _compress(z, w, b):
    bsz, length, g, d = z.shape
    ch = z.reshape(bsz, length // CMP_STRIDE, CMP_STRIDE, g, d)
    head = jnp.einsum('bcjgd,jde->bcge', ch, w[:CMP_STRIDE])
    tail = jnp.einsum('bcjgd,jde->bcge', ch, w[CMP_STRIDE:])
    return head[:, :-1] + tail[:, 1:] + b


def _nsa_core(q, q_rot, qpos, kc, vc, selected, kw, vw, kwpos, gates):
    f32 = jnp.float32
    bsz, t = q.shape[0], q.shape[1]
    hpg = NSA_HEADS // NSA_KV_HEADS
    scale = HEAD_DIM ** -0.5
    qg = q.reshape(bsz, t, NSA_KV_HEADS, hpg, HEAD_DIM)
    qrg = q_rot.reshape(bsz, t, NSA_KV_HEADS, hpg, HEAD_DIM)
    n_cmp = kc.shape[1]
    cmp_end = jnp.arange(n_cmp) * CMP_STRIDE + (CMP_BLOCK - 1)
    m_c = (cmp_end[None, :] <= qpos[:, None])[None, :, None, None, :]
    p_c = _masked_softmax(jnp.einsum('btghd,bngd->btghn', qg, kc).astype(f32) * scale, m_c)
    o_c = jnp.einsum('btghn,bngd->btghd', p_c.astype(vc.dtype), vc)
    per = SLC_BLOCK // CMP_STRIDE
    n_slc = (n_cmp + 1) // per
    imp = jnp.pad(p_c.sum(axis=3), ((0, 0), (0, 0), (0, 0), (0, 1))).reshape(bsz, t, NSA_KV_HEADS, n_slc, per)
    imp = imp.sum(-1) + jnp.pad(imp[..., :-1, per - 1], ((0, 0), (0, 0), (0, 0), (1, 0)))
    blk = jnp.arange(n_slc)[None, :]
    qblk = (qpos // SLC_BLOCK)[:, None]
    valid = blk * SLC_BLOCK <= qpos[:, None]
    forced = (blk == 0) | (blk == qblk) | (blk == qblk - 1)
    score = jnp.where(valid[None, :, None, :], imp + FORCE_BONUS * forced[None, :, None, :], NEG_INF)
    n_top = min(N_SELECT, n_slc)
    _, idx = lax.top_k(score, n_top)
    o_s = selected(qrg, idx)
    dpos = qpos[:, None] - kwpos[None, :]
    m_w = ((dpos >= 0) & (dpos < WINDOW) & (kwpos[None, :] >= 0))[None, :, None, None, :]
    p_w = _masked_softmax(jnp.einsum('btghd,bkgd->btghk', qrg, kw).astype(f32) * scale, m_w)
    o_w = jnp.einsum('btghk,bkgd->btghd', p_w.astype(vw.dtype), vw)
    gt = jax.nn.sigmoid(gates.astype(f32)).astype(q.dtype).reshape(bsz, t, NSA_KV_HEADS, hpg, 3, 1)
    o = gt[..., 0, :] * o_c + gt[..., 1, :] * o_s + gt[..., 2, :] * o_w
    return o.reshape(bsz, t, NSA_WIDTH)


HPG = NSA_HEADS // NSA_KV_HEADS
CMP_PER_SLC = SLC_BLOCK // CMP_STRIDE
SEL_CHUNK = 512
WIN_KEYS = WINDOW + Q_BLOCK
DROPPED = -3e38


def _softmax_rows(s, ok):
    sm = jnp.where(ok[None], s, NEG_INF)
    e = jnp.exp(sm - jnp.max(sm, axis=-1, keepdims=True))
    p = e / jnp.sum(e, axis=-1, keepdims=True)
    return jnp.where(ok[None], p, 0.0)


def _nsa_prompt_kernel(q_ref, qr_ref, g_ref, kct_ref, vc_ref, kst_ref, vs_ref, kwt_ref, vw_ref,
                       impt_ref, exp_ref, o_ref, *, n_cmp):
    f32, bf = jnp.float32, jnp.bfloat16
    i = pl.program_id(2)
    start = i * Q_BLOCK
    rows = HPG * Q_BLOCK
    n_cpad = kct_ref.shape[-1]
    n_slc = impt_ref.shape[0]
    scale = HEAD_DIM ** -0.5
    q = (q_ref[0, 0, 0] * scale).astype(bf)
    qr = (qr_ref[0, 0, 0] * scale).astype(bf)
    t_col = start + lax.broadcasted_iota(jnp.int32, (Q_BLOCK, 1), 0)

    s = jnp.dot(q, kct_ref[0, 0], preferred_element_type=f32).reshape(HPG, Q_BLOCK, n_cpad)
    n_io = lax.broadcasted_iota(jnp.int32, (Q_BLOCK, n_cpad), 1)
    ok_c = jnp.where(n_io < n_cmp, n_io * CMP_STRIDE + (CMP_BLOCK - 1), 2 ** 30) <= t_col
    p = _softmax_rows(s, ok_c)
    o_c = jnp.dot(p.reshape(rows, n_cpad).astype(bf), vc_ref[0, 0], preferred_element_type=f32)

    psum = p[0] + p[1] + p[2] + p[3]
    hi = psum.astype(bf)
    r1 = psum - hi.astype(f32)
    mid = r1.astype(bf)
    lo = (r1 - mid.astype(f32)).astype(bf)
    nt = (((1,), (1,)), ((), ()))
    imp_t = (lax.dot_general(impt_ref[...], hi, nt, preferred_element_type=f32)
             + lax.dot_general(impt_ref[...], mid, nt, preferred_element_type=f32)
             + lax.dot_general(impt_ref[...], lo, nt, preferred_element_type=f32))
    j_io = lax.broadcasted_iota(jnp.int32, (n_slc, Q_BLOCK), 0)
    t_row = start + lax.broadcasted_iota(jnp.int32, (n_slc, Q_BLOCK), 1)
    qblk = t_row // SLC_BLOCK
    forced = jnp.where(j_io == 0, 1.0, 0.0) + jnp.where(j_io == qblk, 1.0, 0.0) + jnp.where(j_io == qblk - 1, 1.0, 0.0)
    forced = jnp.minimum(forced, 1.0)
    score = jnp.where(j_io * SLC_BLOCK <= t_row, imp_t + FORCE_BONUS * forced, NEG_INF)

    def pick(_, carry):
        sc, sel = carry
        best = jnp.max(sc, axis=0, keepdims=True)
        first = jnp.min(jnp.where(sc == best, j_io, n_slc), axis=0, keepdims=True)
        hit = j_io == first
        return jnp.where(hit, DROPPED, sc), jnp.where(hit, 1.0, sel)

    _, sel_t = lax.fori_loop(0, min(N_SELECT, n_slc), pick, (score, jnp.zeros((n_slc, Q_BLOCK), f32)))
    sel = sel_t.T.astype(bf)

    def chunk(c, carry):
        m, l, acc = carry
        off = pl.multiple_of(c * SEL_CHUNK, SEL_CHUNK)
        kt = kst_ref[0, 0, :, pl.ds(off, SEL_CHUNK)]
        v = vs_ref[0, 0, pl.ds(off, SEL_CHUNK), :]
        sc = jnp.dot(qr, kt, preferred_element_type=f32).reshape(HPG, Q_BLOCK, SEL_CHUNK)
        chosen = jnp.dot(sel, exp_ref[:, pl.ds(off, SEL_CHUNK)], preferred_element_type=f32)
        kpos = off + lax.broadcasted_iota(jnp.int32, (Q_BLOCK, SEL_CHUNK), 1)
        ok = jnp.where(kpos <= t_col, chosen, 0.0) > 0.5
        sc = jnp.where(ok[None], sc, NEG_INF)
        m_new = jnp.maximum(m, jnp.max(sc, axis=-1, keepdims=True))
        alpha = jnp.exp(m - m_new)
        pe = jnp.exp(sc - m_new)
        l = alpha * l + jnp.sum(pe, axis=-1, keepdims=True)
        pv = jnp.dot(pe.reshape(rows, SEL_CHUNK).astype(bf), v, preferred_element_type=f32)
        return m_new, l, alpha * acc + pv.reshape(HPG, Q_BLOCK, HEAD_DIM)

    n_chunks = (start + Q_BLOCK + SEL_CHUNK - 1) // SEL_CHUNK
    init = (jnp.full((HPG, Q_BLOCK, 1), NEG_INF, f32), jnp.zeros((HPG, Q_BLOCK, 1), f32),
            jnp.zeros((HPG, Q_BLOCK, HEAD_DIM), f32))
    _, l_s, acc_s = lax.fori_loop(0, n_chunks, chunk, init)
    o_s = (acc_s / l_s).reshape(rows, HEAD_DIM)

    n_keys = kwt_ref.shape[-1]
    wk = min(WIN_KEYS, n_keys)
    k0 = pl.multiple_of(jnp.maximum(start + Q_BLOCK - wk, 0), Q_BLOCK)
    kt = kwt_ref[0, 0, :, pl.ds(k0, wk)]
    v = vw_ref[0, 0, pl.ds(k0, wk), :]
    sw = jnp.dot(qr, kt, preferred_element_type=f32).reshape(HPG, Q_BLOCK, wk)
    dpos = t_col - (k0 + lax.broadcasted_iota(jnp.int32, (Q_BLOCK, wk), 1))
    ok_w = jnp.where(dpos >= 0, dpos, WINDOW) < WINDOW
    pw = _softmax_rows(sw, ok_w)
    o_w = jnp.dot(pw.reshape(rows, wk).astype(bf), v, preferred_element_type=f32)

    gt = jax.nn.sigmoid(g_ref[0, 0])
    heads = []
    for h in range(HPG):
        r = slice(h * Q_BLOCK, (h + 1) * Q_BLOCK)
        heads.append(gt[:, 3 * h:3 * h + 1] * o_c[r] + gt[:, 3 * h + 1:3 * h + 2] * o_s[r]
                     + gt[:, 3 * h + 2:3 * h + 3] * o_w[r])
    o_ref[0] = jnp.concatenate(heads, axis=1).astype(o_ref.dtype)


def _nsa_prep_kernel(q_ref, kv_ref, cos_ref, sin_ref, qb_ref, qrb_ref, kvr_ref, kst_ref, kwt_ref, vs_ref, vw_ref):
    cos, sin = cos_ref[...], sin_ref[...]
    lane = lax.broadcasted_iota(jnp.int32, (Q_BLOCK, LANES), 1)
    first_half = (lane % HEAD_DIM) < HEAD_DIM // 2

    def rope(x):
        partner = jnp.where(first_half, pltpu.roll(x, LANES - HEAD_DIM // 2, axis=1),
                            pltpu.roll(x, HEAD_DIM // 2, axis=1))
        return x * cos + partner * sin

    for c in range(NSA_WIDTH // LANES):
        x = q_ref[:, c * LANES:(c + 1) * LANES]
        xr = rope(x)
        for e in range(LANES // HEAD_DIM):
            g, h = divmod(c * (LANES // HEAD_DIM) + e, HPG)
            rows = slice(h * Q_BLOCK, (h + 1) * Q_BLOCK)
            qb_ref[g, rows, :] = x[:, e * HEAD_DIM:(e + 1) * HEAD_DIM].astype(BF16)
            qrb_ref[g, rows, :] = xr[:, e * HEAD_DIM:(e + 1) * HEAD_DIM].astype(BF16)

    kind_w = NSA_KV_HEADS * HEAD_DIM
    for kind in range(6):
        for c in range(kind_w // LANES):
            col = kind * kind_w + c * LANES
            x = kv_ref[:, col:col + LANES]
            if kind in (2, 4):
                x = rope(x)
                t_ref = kst_ref if kind == 2 else kwt_ref
                t_ref[c * LANES:(c + 1) * LANES, :] = x.T.astype(BF16)
            if kind in (3, 5):
                v_ref = vs_ref if kind == 3 else vw_ref
                for e in range(LANES // HEAD_DIM):
                    v_ref[c * (LANES // HEAD_DIM) + e] = x[:, e * HEAD_DIM:(e + 1) * HEAD_DIM].astype(BF16)
            kvr_ref[:, col:col + LANES] = x


def _rope_tables(pos):
    half = HEAD_DIM // 2
    inv = ROPE_THETA ** (-jnp.arange(half, dtype=jnp.float32) / half)
    ang = pos.astype(jnp.float32)[:, None] * inv[None, :]
    cos, sin = jnp.cos(ang), jnp.sin(ang)
    reps = LANES // HEAD_DIM
    return jnp.tile(jnp.concatenate([cos, cos], axis=1), (1, reps)), jnp.tile(jnp.concatenate([-sin, sin], axis=1), (1, reps))


def _nsa_prompt(q, kv, gates, phi, phi_b, bsz, s):
    g, d = NSA_KV_HEADS, HEAD_DIM
    n_qb = s // Q_BLOCK
    rows = HPG * Q_BLOCK
    kind_w = g * d
    cos, sin = _rope_tables(jnp.arange(s))
    tile = lambda w: pl.BlockSpec((Q_BLOCK, w), lambda b, i: (b * n_qb + i, 0))
    tab = pl.BlockSpec((Q_BLOCK, LANES), lambda b, i: (i, 0))
    qb_spec = pl.BlockSpec((None, g, None, rows, d), lambda b, i: (b, 0, i, 0, 0))
    kt_spec = pl.BlockSpec((None, kind_w, Q_BLOCK), lambda b, i: (b, 0, i))
    v_spec = pl.BlockSpec((None, g, Q_BLOCK, d), lambda b, i: (b, 0, i, 0))
    qb_shape = jax.ShapeDtypeStruct((bsz, g, n_qb, rows, d), BF16)
    kt_shape = jax.ShapeDtypeStruct((bsz, kind_w, s), BF16)
    v_shape = jax.ShapeDtypeStruct((bsz, g, s, d), BF16)
    qb, qrb, kv_rot, kst, kwt, vs, vw = pl.pallas_call(
        _nsa_prep_kernel,
        grid=(bsz, n_qb),
        in_specs=[tile(NSA_WIDTH), tile(N_KV_COLS), tab, tab],
        out_specs=[qb_spec, qb_spec, tile(N_KV_COLS), kt_spec, kt_spec, v_spec, v_spec],
        out_shape=[qb_shape, qb_shape, jax.ShapeDtypeStruct(kv.shape, jnp.float32), kt_shape, kt_shape, v_shape, v_shape],
        compiler_params=_params("arbitrary", "arbitrary"),
        name="nsa_prep",
    )(q, kv, cos, sin)

    pages = jnp.arange(bsz * n_qb, dtype=jnp.int32).reshape(bsz, n_qb)
    ht = _past_compress(kv.reshape(1, bsz * n_qb, Q_BLOCK, N_KV_COLS), 0, pages, _pair_weights(phi))
    ht = ht.reshape(bsz, ht.shape[1], 2, g, 2, d)
    n_cmp = ht.shape[1] - 1
    n_cpad = -(-(n_cmp + 1) // LANES) * LANES
    n_slc = (n_cmp + 1) // CMP_PER_SLC
    pad_c = ((0, 0), (0, n_cpad - n_cmp), (0, 0), (0, 0))
    kc = jnp.pad(ht[:, :-1, 0, :, 0] + ht[:, 1:, 0, :, 1] + phi_b[0], pad_c)
    vc = jnp.pad(ht[:, :-1, 1, :, 0] + ht[:, 1:, 1, :, 1] + phi_b[1], pad_c)
    kct = kc.transpose(0, 2, 3, 1).astype(BF16)
    vcg = vc.transpose(0, 2, 1, 3).astype(BF16)
    gates_g = gates[:, :N_NSA_GATES].reshape(bsz, s, g, 3 * HPG).transpose(0, 2, 1, 3)

    n_io = jnp.arange(n_cpad)[None, :]
    j_io = jnp.arange(n_slc)[:, None]
    imp_t = ((n_io >= CMP_PER_SLC * j_io - 1) & (n_io < CMP_PER_SLC * (j_io + 1))).astype(BF16)
    expand = (jnp.arange(s)[None, :] // SLC_BLOCK == j_io).astype(BF16)
    qspec = pl.BlockSpec((1, 1, 1, rows, d), lambda b, gg, i: (b, gg, i, 0, 0))

    def whole(shape):
        return pl.BlockSpec((1, 1) + shape, lambda b, gg, i: (b, gg, 0, 0))

    o = pl.pallas_call(
        functools.partial(_nsa_prompt_kernel, n_cmp=n_cmp),
        grid=(bsz, g, n_qb),
        in_specs=[qspec, qspec,
                  pl.BlockSpec((1, 1, Q_BLOCK, 3 * HPG), lambda b, gg, i: (b, gg, i, 0)),
                  whole((d, n_cpad)), whole((n_cpad, d)),
                  whole((d, s)), whole((s, d)), whole((d, s)), whole((s, d)),
                  pl.BlockSpec((n_slc, n_cpad), lambda b, gg, i: (0, 0)),
                  pl.BlockSpec((n_slc, s), lambda b, gg, i: (0, 0))],
        out_specs=pl.BlockSpec((1, Q_BLOCK, HPG * d), lambda b, gg, i: (b, i, gg)),
        out_shape=jax.ShapeDtypeStruct((bsz, s, NSA_WIDTH), BF16),
        compiler_params=_params("arbitrary", "arbitrary", "arbitrary"),
        name="nsa_prompt",
    )(qb, qrb, gates_g, kct, vcg, kst.reshape(bsz, g, d, s), vs, kwt.reshape(bsz, g, d, s), vw, imp_t, expand)

    n_win = min(WINDOW, s)
    kv_rot = kv_rot.reshape(bsz, s, 6, g, d)
    return o.reshape(bsz * s, NSA_WIDTH), kv_rot[:, :, :4], kv_rot[:, s - n_win:, 4:]


def _pair_weights(phi):
    d = HEAD_DIM
    ht = jnp.concatenate([phi[:, :CMP_STRIDE], phi[:, CMP_STRIDE:]], axis=-1)
    z = jnp.zeros_like(ht)
    return jnp.concatenate([jnp.concatenate([ht, z], axis=-1), jnp.concatenate([z, ht], axis=-1)], axis=-2).astype(BF16)


def _sample_compressed(cache, l, page_table, new_rows, phi, phi_b):
    bsz = page_table.shape[0]
    g, d = NSA_KV_HEADS, HEAD_DIM
    pool = cache.reshape(cache.shape[0], cache.shape[1], PAGE_SIZE, 4 * g * d)
    ht = _past_compress(pool, l, page_table, _pair_weights(phi))
    ht = ht.reshape(bsz, ht.shape[1], 2, g, 2, d)
    out = []
    for kind in range(2):
        ch = _pad_rows(new_rows[kind], SLC_BLOCK)
        ch = ch.reshape(bsz, ch.shape[1] // CMP_STRIDE, CMP_STRIDE, g, d)
        head = jnp.concatenate([ht[:, :, kind, :, 0], jnp.einsum('bcjgd,jde->bcge', ch, phi[kind, :CMP_STRIDE])], axis=1)
        tail = jnp.concatenate([ht[:, :, kind, :, 1], jnp.einsum('bcjgd,jde->bcge', ch, phi[kind, CMP_STRIDE:])], axis=1)
        out.append(head[:, :-1] + tail[:, 1:] + phi_b[kind])
    return out


def _nsa_sample(q, kv, gates, phi, phi_b, cache, l, page_table, win_buf):
    bsz, t = q.shape[0], q.shape[1]
    pos = PAST_LEN + jnp.arange(t)
    k_cmp, v_cmp, k_slc, v_slc, k_win, v_win = [kv[:, :, i] for i in range(6)]
    q_rot, k_slc, k_win = _rope(q, pos), _rope(k_slc, pos), _rope(k_win, pos)
    kc, vc = _sample_compressed(cache, l, page_table, (k_cmp, v_cmp), phi, phi_b)
    kw = jnp.concatenate([win_buf[:, :, 0].astype(q.dtype), k_win], axis=1)
    vw = jnp.concatenate([win_buf[:, :, 1].astype(q.dtype), v_win], axis=1)
    n_win = win_buf.shape[1]
    kwpos = PAST_LEN - n_win + jnp.arange(n_win + t)

    def selected(q_rot_g, idx):
        return _sample_selected(q_rot_g, idx, pos, cache, l, page_table, k_slc, v_slc)

    o = _nsa_core(q, q_rot, pos, kc, vc, selected, kw, vw, kwpos, gates)
    rows = jnp.stack([k_cmp, v_cmp, k_slc, v_slc], axis=2)
    win_rows = jnp.stack([k_win, v_win], axis=2)
    return o, rows, win_rows


def _sample_selected_kernel(pt_ref, *refs):
    pages = refs[:PAGES_PER_STEP]
    q_ref, ok_ref, new_ref, oknew_ref, o_ref, m_s, l_s, acc_s = refs[PAGES_PER_STEP:]
    step = pl.program_id(1)
    kw = NSA_KV_HEADS * HEAD_DIM

    @pl.when(step == 0)
    def _():
        m_s[...] = jnp.full_like(m_s, NEG_INF)
        l_s[...] = jnp.zeros_like(l_s)
        acc_s[...] = jnp.zeros_like(acc_s)

    def update(kv, ok):
        k = kv[:, :kw].astype(BF16)
        v = kv[:, kw:].astype(BF16)
        sc = lax.dot_general(q_ref[...], k, (((1,), (1,)), ((), ())), preferred_element_type=jnp.float32)
        sc = jnp.where(ok > 0.5, sc, NEG_INF)
        m_new = jnp.maximum(m_s[...], jnp.max(sc, axis=-1, keepdims=True))
        alpha = jnp.exp(m_s[...] - m_new)
        p = jnp.exp(sc - m_new)
        l_s[...] = alpha * l_s[...] + jnp.sum(p, axis=-1, keepdims=True)
        acc_s[...] = alpha * acc_s[...] + jnp.dot(p.astype(BF16), v, preferred_element_type=jnp.float32)
        m_s[...] = m_new

    update(jnp.concatenate([p[...] for p in pages], axis=0), ok_ref[...])

    @pl.when(step == pl.num_programs(1) - 1)
    def _():
        update(new_ref[...], oknew_ref[...])
        o_ref[...] = acc_s[...] / l_s[...]


def _sample_selected(q_rot_g, idx, qpos, cache, l, page_table, k_new, v_new):
    bsz, t = q_rot_g.shape[0], q_rot_g.shape[1]
    g, d = NSA_KV_HEADS, HEAD_DIM
    n_pages = page_table.shape[1]
    past = n_pages * PAGE_SIZE
    rows = g * t * HPG
    kw = g * d
    n_steps = n_pages // PAGES_PER_STEP
    keys_per_step = PAGES_PER_STEP * PAGE_SIZE
    pool = cache.reshape(cache.shape[0], cache.shape[1], PAGE_SIZE, 4 * kw)

    qg = (q_rot_g * HEAD_DIM ** -0.5).transpose(0, 2, 1, 3, 4).reshape(bsz, g, t * HPG, d)
    q_bd = (qg[:, :, :, None, :] * jnp.eye(g, dtype=qg.dtype)[None, :, None, :, None]).reshape(bsz, rows, kw).astype(BF16)

    n_blk = past // SLC_BLOCK
    chosen = (idx[..., None] == jnp.arange(n_blk + 1)).any(axis=-2)

    def by_row(z):
        z = jnp.broadcast_to(z.transpose(0, 2, 1, 3)[:, :, :, None, :], (bsz, g, t, HPG, z.shape[-1]))
        return z.reshape(bsz, rows, z.shape[-1]).astype(jnp.float32)

    ok_past = by_row(jnp.repeat(chosen[..., :n_blk], SLC_BLOCK, axis=-1))
    j = jnp.arange(PAGE_SIZE)
    new_ok = (j[None, :] < t) & (past + j[None, :] <= qpos[:, None])
    ok_new = by_row(chosen[..., n_blk:] & new_ok[None, :, None, :])
    new_kv = jnp.concatenate([k_new.reshape(bsz, t, kw), v_new.reshape(bsz, t, kw)], axis=-1)
    new_kv = jnp.pad(new_kv, ((0, 0), (0, PAGE_SIZE - t), (0, 0)))

    def page_spec(jj):
        return pl.BlockSpec((None, None, PAGE_SIZE, 2 * kw),
                            lambda b, s, pt: (l, pt[b, s * PAGES_PER_STEP + jj], 0, 1))

    o = pl.pallas_call(
        _sample_selected_kernel,
        grid_spec=pltpu.PrefetchScalarGridSpec(
            num_scalar_prefetch=1,
            grid=(bsz, n_steps),
            in_specs=[page_spec(jj) for jj in range(PAGES_PER_STEP)]
            + [pl.BlockSpec((None, rows, kw), lambda b, s, pt: (b, 0, 0)),
               pl.BlockSpec((None, rows, keys_per_step), lambda b, s, pt: (b, 0, s)),
               pl.BlockSpec((None, PAGE_SIZE, 2 * kw), lambda b, s, pt: (b, 0, 0)),
               pl.BlockSpec((None, rows, PAGE_SIZE), lambda b, s, pt: (b, 0, 0))],
            out_specs=pl.BlockSpec((None, rows, kw), lambda b, s, pt: (b, 0, 0)),
            scratch_shapes=[pltpu.VMEM((rows, 1), jnp.float32), pltpu.VMEM((rows, 1), jnp.float32),
                            pltpu.VMEM((rows, kw), jnp.float32)],
        ),
        out_shape=jax.ShapeDtypeStruct((bsz, rows, kw), jnp.float32),
        compiler_params=_params("arbitrary", "arbitrary"),
        name="sample_selected",
    )(page_table, *([pool] * PAGES_PER_STEP), q_bd, ok_past, new_kv, ok_new)
    o = o.reshape(bsz, g, t, HPG, g, d)
    o = jnp.stack([o[:, gg, :, :, gg] for gg in range(g)], axis=1)
    return o.transpose(0, 2, 1, 3, 4)


LANES = 128
WKV_T_CHUNK = 64


def _wkv_kernel(w_ref, kk_ref, kka_ref, k_ref, r_ref, v_ref, s0_ref, y_ref, sfin_ref, s_scr):
    c = pl.program_id(0)
    n_vg = s_scr.shape[0]

    @pl.when(c == 0)
    def _():
        s_scr[...] = s0_ref[...]

    def step(t, carry):
        w, kk, kka, k, r = w_ref[t], kk_ref[t], kka_ref[t], k_ref[t], r_ref[t]
        for vg in range(n_vg):
            s = s_scr[vg]
            sa = jnp.sum(s * kk, axis=0, keepdims=True)
            s = s * w - kka * sa + k * v_ref[t, vg:vg + 1, :]
            s_scr[vg] = s
            y_ref[t, vg:vg + 1, :] = jnp.sum(s * r, axis=0, keepdims=True)
        return carry

    lax.fori_loop(0, w_ref.shape[0], step, 0)

    @pl.when(c == pl.num_programs(0) - 1)
    def _():
        sfin_ref[...] = s_scr[...]


def _wkv_scan(r, w, k, v, kk, kka, s0):
    bsz, t, h, n = r.shape
    bh = bsz * h
    vrep = LANES // bh
    n_vg = n // vrep
    tc = min(WKV_T_CHUNK, t)

    def key_tiles(z):
        z = z.transpose(1, 3, 0, 2).reshape(t, n, 1, bh)
        return jnp.broadcast_to(z, (t, n, vrep, bh)).reshape(t, n, LANES)

    v_rows = v.transpose(1, 3, 0, 2).reshape(t, n_vg, LANES)
    s_tiles = s0.transpose(2, 3, 0, 1).reshape(n_vg, vrep, n, bh).transpose(0, 2, 1, 3).reshape(n_vg, n, LANES)
    kspec = pl.BlockSpec((tc, n, LANES), lambda c: (c, 0, 0))
    vspec = pl.BlockSpec((tc, n_vg, LANES), lambda c: (c, 0, 0))
    sspec = pl.BlockSpec((n_vg, n, LANES), lambda c: (0, 0, 0))
    y, s_fin = pl.pallas_call(
        _wkv_kernel,
        grid=(t // tc,),
        in_specs=[kspec] * 5 + [vspec, sspec],
        out_specs=[vspec, sspec],
        out_shape=[jax.ShapeDtypeStruct((t, n_vg, LANES), jnp.float32),
                   jax.ShapeDtypeStruct((n_vg, n, LANES), jnp.float32)],
        scratch_shapes=[pltpu.VMEM((n_vg, n, LANES), jnp.float32)],
        compiler_params=pltpu.CompilerParams(dimension_semantics=("arbitrary",),
                                             vmem_limit_bytes=48 * 1024 * 1024),
        name="wkv_scan",
    )(key_tiles(w), key_tiles(kk), key_tiles(kka), key_tiles(k), key_tiles(r), v_rows, s_tiles)
    y = y.reshape(t, n, bsz, h).transpose(2, 0, 3, 1)
    s_fin = s_fin.reshape(n_vg, n, vrep, bsz, h).transpose(3, 4, 0, 2, 1).reshape(bsz, h, n, n)
    return y, s_fin


def _rwkv7(c, wkv0, shift0, mu, w0, w2, a0, a2, g2, k_k, k_a, r_k, ln_g, ln_b):
    f32 = jnp.float32
    bsz, t, _ = c.shape
    prev = jnp.concatenate([shift0.astype(c.dtype), c[:, :-1]], axis=1)
    cm = c + mu * (prev - c)
    r, k, v, wl, al, gl = jnp.split(cm, RWKV_SPLITS, axis=-1)
    log_w = -jax.nn.softplus(-(w0 + jnp.tanh(wl) @ w2).astype(f32)) - 0.5
    decay = jnp.exp(-jnp.exp(log_w))
    a = jax.nn.sigmoid((a0 + al @ a2).astype(f32))
    g = jax.nn.sigmoid(gl) @ g2

    def heads(z):
        return z.astype(f32).reshape(bsz, t, RWKV_HEADS, RWKV_HEAD_DIM)

    kk = heads(k * k_k)
    kk = kk * lax.rsqrt(jnp.sum(kk * kk, axis=-1, keepdims=True) + 1e-12)
    k = k.astype(f32) * (1.0 + (a - 1.0) * k_a.astype(f32))
    rh, kh, vh, wh, ah = heads(r), heads(k), heads(v), heads(decay), heads(a)

    y, s_fin = _wkv_scan(rh, wh, kh, vh, kk, kk * ah, wkv0.astype(f32))
    mean = jnp.mean(y, axis=-1, keepdims=True)
    var = jnp.mean(jnp.square(y - mean), axis=-1, keepdims=True)
    y = ((y - mean) * lax.rsqrt(var + RWKV_GN_EPS)).reshape(bsz, t, RWKV_WIDTH) * ln_g.astype(f32) + ln_b.astype(f32)
    bonus = (jnp.sum(rh * kh * r_k.astype(f32), axis=-1, keepdims=True) * vh).reshape(bsz, t, RWKV_WIDTH)
    out = ((y + bonus) * g.astype(f32)).astype(c.dtype)
    return out, s_fin.astype(c.dtype), c[:, t - 1:]


def _pad_cols(w, n):
    return jnp.pad(w, ((0, 0), (0, 0), (0, n - w.shape[2])))


def _split_w_in(w_in):
    s = (0,) + IN_SPLITS + (D_IN,)
    seg = [w_in[:, :, s[i]:s[i + 1]] for i in range(6)]
    seg[3] = _pad_cols(seg[3], LANES)
    seg[4] = _pad_cols(seg[4], RW_PAD)
    return [z.astype(BF16) for z in seg]


RW_PAD = 2 * 14 * LANES
IN_TILES = (1024, 1024, N_KV_COLS, LANES, RW_PAD // 2, 1024)


def _layer(x, h, l, P, st, nsa_fn, fuse_ffn):
    lru_h0, lru_conv0, wkv0, shift0, ffn_conv0 = st
    bsz, t, _ = x.shape
    m = bsz * t
    norms = P['norms']
    xa, q, kv, nsa_g, rw, mg = [_mm_ws(h, w, l, tn) for w, tn in zip(P['w_in_seg'], IN_TILES)]
    rw = rw[:, :RWKV_COLS].reshape(bsz, t, RWKV_COLS)
    lru_fn = _lru_prompt if fuse_ffn else _lru_glue
    o_a, lru_h, lru_conv = lru_fn(xa, lru_h0, lru_conv0, P['lru_conv_w'][l], P['lru_conv_b'][l], P['lru_gate_w'][l], P['lru_gate_b'][l], P['lru_lambda'][l], bsz, t)
    o_b, nsa_rows, win_rows = nsa_fn(l, q, kv, nsa_g, bsz, t)
    o_c, wkv, shift = _rwkv7(rw, wkv0, shift0, P['rwkv_mu'][l], P['rwkv_w0'][l], P['rwkv_w2'][l], P['rwkv_a0'][l], P['rwkv_a2'][l], P['rwkv_g2'][l], P['rwkv_k_k'][l], P['rwkv_k_a'][l], P['rwkv_r_k'][l], P['rwkv_ln_g'][l], P['rwkv_ln_b'][l])

    def rows_bf(z):
        return z.reshape(m, BRANCH_WIDTH).astype(BF16)

    merged = _branch_merge(rows_bf(o_a), rows_bf(o_b), rows_bf(o_c), P['w_branch'], l, mg)
    x2, h2 = _mm_norm_res(merged, P['w_out_bf'], l, x.reshape(m, D_MODEL), norms[l, 1][None], norms[l, 2][None])
    if fuse_ffn:
        conv_w = P['ffn_conv_w'][l].reshape(FFN_CONV, 2, D_FF)
        conv_b = P['ffn_conv_b'][l].reshape(2, D_FF)
        act, ffn_conv = _ffn_up_act(h2, P['w_up'], l, conv_w, conv_b,
                                    ffn_conv0.reshape(bsz, FFN_CONV - 1, 2, D_FF), t)
        ffn_conv = ffn_conv.reshape(bsz, FFN_CONV - 1, 2 * D_FF)
    else:
        u = _mm_ws(h2, P['w_up'], l, 1024).reshape(bsz, t, 2 * D_FF)
        u, ffn_conv = _causal_conv(u, ffn_conv0, P['ffn_conv_w'][l], P['ffn_conv_b'][l])
        u_gate, u_val = jnp.split(u, 2, axis=-1)
        act = (jax.nn.gelu(u_gate) * u_val).reshape(m, D_FF).astype(BF16)
    g_next = norms[min(l + 1, DEPTH - 1), 0][None]
    x3, h_next = _mm_norm_res(act, P['w_down_bf'], l, x2, norms[l, 3][None], g_next)
    return x3.reshape(bsz, t, D_MODEL), h_next, (nsa_rows, win_rows, lru_h, lru_conv, wkv, shift, ffn_conv)


def kernel(x_prompt, x_sample, cache_nsa, cache_win, state_lru_h, state_lru_conv, state_rwkv_wkv, state_rwkv_shift, state_ffn_conv, page_table, norms, w_in, lru_conv_w, lru_conv_b, lru_gate_w, lru_gate_b, lru_lambda, nsa_phi, nsa_phi_b, rwkv_mu, rwkv_w0, rwkv_w2, rwkv_a0, rwkv_a2, rwkv_g2, rwkv_k_k, rwkv_k_a, rwkv_r_k, rwkv_ln_g, rwkv_ln_b, w_branch, w_out, w_up, ffn_conv_w, ffn_conv_b, w_down):
    P = {'norms': norms, 'w_in': w_in, 'lru_conv_w': lru_conv_w, 'lru_conv_b': lru_conv_b, 'lru_gate_w': lru_gate_w, 'lru_gate_b': lru_gate_b, 'lru_lambda': lru_lambda, 'rwkv_mu': rwkv_mu, 'rwkv_w0': rwkv_w0, 'rwkv_w2': rwkv_w2, 'rwkv_a0': rwkv_a0, 'rwkv_a2': rwkv_a2, 'rwkv_g2': rwkv_g2, 'rwkv_k_k': rwkv_k_k, 'rwkv_k_a': rwkv_k_a, 'rwkv_r_k': rwkv_r_k, 'rwkv_ln_g': rwkv_ln_g, 'rwkv_ln_b': rwkv_ln_b, 'w_branch': w_branch, 'w_out': w_out, 'w_up': w_up, 'ffn_conv_w': ffn_conv_w, 'ffn_conv_b': ffn_conv_b, 'w_down': w_down}

    def nsa_prompt_fn(l, q, kv, g, bsz, t):
        return _nsa_prompt(q, kv, g, nsa_phi[l], nsa_phi_b[l], bsz, t)

    def nsa_sample_fn(l, q, kv, g, bsz, t):
        q = q.reshape(bsz, t, NSA_HEADS, HEAD_DIM)
        kv = kv.reshape(bsz, t, 6, NSA_KV_HEADS, HEAD_DIM)
        g = g[:, :N_NSA_GATES].reshape(bsz, t, N_NSA_GATES)
        return _nsa_sample(q, kv, g, nsa_phi[l], nsa_phi_b[l], cache_nsa, l, page_table, cache_win[l])

    P['w_in_seg'] = _split_w_in(w_in)
    P['w_out_bf'] = w_out.astype(BF16)
    P['w_down_bf'] = w_down.astype(BF16)

    def first_norm(x):
        return _rmsnorm(x, norms[0, 0]).reshape(-1, D_MODEL).astype(BF16)

    bsz, dt = x_prompt.shape[0], x_prompt.dtype
    zero_state = (jnp.zeros((bsz, LRU_WIDTH), dt), jnp.zeros((bsz, LRU_CONV - 1, LRU_WIDTH), dt), jnp.zeros((bsz, RWKV_HEADS, RWKV_HEAD_DIM, RWKV_HEAD_DIM), dt), jnp.zeros((bsz, 1, RWKV_COLS), dt), jnp.zeros((bsz, FFN_CONV - 1, 2 * D_FF), dt))
    y_p, y_s = x_prompt, x_sample
    h_p, h_s = first_norm(x_prompt), first_norm(x_sample)
    new_p, new_s = [], []
    for l in range(DEPTH):
        y_p, h_p, st_p = _layer(y_p, h_p, l, P, zero_state, nsa_prompt_fn, True)
        y_s, h_s, st_s = _layer(y_s, h_s, l, P, (state_lru_h[l], state_lru_conv[l], state_rwkv_wkv[l], state_rwkv_shift[l], state_ffn_conv[l]), nsa_sample_fn, False)
        new_p.append(st_p)
        new_s.append(st_s)

    def stacked(rows, i):
        return jnp.stack([r[i] for r in rows])

    return (y_p, y_s) + tuple(stacked(new_p, i) for i in range(7)) + tuple(stacked(new_s, i) for i in range(7))
```

```python
import functools

import jax
import jax.numpy as jnp
from jax import lax
from jax.experimental import pallas as pl
from jax.experimental.pallas import tpu as pltpu

D_MODEL = 2048
DEPTH = 4
PAST_LEN = 16384
PAGE_SIZE = 128
NORM_EPS = 1e-6
BRANCH_WIDTH = 1024
LRU_WIDTH = BRANCH_WIDTH
LRU_BLOCKS = 16
LRU_BLOCK_DIM = LRU_WIDTH // LRU_BLOCKS
LRU_CONV = 4
LRU_C = 8.0
NSA_HEADS = 16
NSA_KV_HEADS = 4
HEAD_DIM = 64
NSA_WIDTH = NSA_HEADS * HEAD_DIM
CMP_STRIDE = 16
CMP_BLOCK = 2 * CMP_STRIDE
SLC_BLOCK = 64
N_SELECT = 16
WINDOW = 512
Q_BLOCK = 128
ROPE_THETA = 10000.0
FORCE_BONUS = 1000.0
NEG_INF = -1e30
RWKV_HEADS = 16
RWKV_HEAD_DIM = 64
RWKV_WIDTH = RWKV_HEADS * RWKV_HEAD_DIM
DECAY_LORA = 64
ICL_LORA = 64
GATE_LORA = 160
RWKV_GN_EPS = 64e-5
D_FF = 3 * D_MODEL
FFN_CONV = 3
N_KV_COLS = 6 * NSA_KV_HEADS * HEAD_DIM
N_NSA_GATES = 3 * NSA_HEADS
RWKV_COLS = 3 * RWKV_WIDTH + DECAY_LORA + ICL_LORA + GATE_LORA
IN_SPLITS = (LRU_WIDTH, LRU_WIDTH + NSA_WIDTH, LRU_WIDTH + NSA_WIDTH + N_KV_COLS,
             LRU_WIDTH + NSA_WIDTH + N_KV_COLS + N_NSA_GATES,
             LRU_WIDTH + NSA_WIDTH + N_KV_COLS + N_NSA_GATES + RWKV_COLS)
D_IN = IN_SPLITS[-1] + 3 * D_MODEL
RWKV_SPLITS = (RWKV_WIDTH, 2 * RWKV_WIDTH, 3 * RWKV_WIDTH, 3 * RWKV_WIDTH + DECAY_LORA,
               3 * RWKV_WIDTH + DECAY_LORA + ICL_LORA)


def _mm_kernel(x_ref, w_ref, o_ref):
    @pl.when(pl.program_id(2) == 0)
    def _():
        o_ref[...] = jnp.zeros_like(o_ref)

    o_ref[...] += jnp.dot(x_ref[...].astype(jnp.bfloat16), w_ref[...].astype(jnp.bfloat16),
                          preferred_element_type=jnp.float32)


def _mm(x, w, tm=512, tn=512, tk=1024):
    m, k = x.shape
    n = w.shape[1]
    tm = min(tm, m)
    tk = min(tk, k)
    assert m % tm == 0 and k % tk == 0
    return pl.pallas_call(
        _mm_kernel,
        grid=(m // tm, pl.cdiv(n, tn), k // tk),
        in_specs=[pl.BlockSpec((tm, tk), lambda i, j, kk: (i, kk)),
                  pl.BlockSpec((tk, tn), lambda i, j, kk: (kk, j))],
        out_specs=pl.BlockSpec((tm, tn), lambda i, j, kk: (i, j)),
        out_shape=jax.ShapeDtypeStruct((m, n), jnp.float32),
        compiler_params=pltpu.CompilerParams(
            dimension_semantics=("parallel", "parallel", "arbitrary")),
        name="mm",
    )(x, w)


def _mm3(x, w):
    b, t, k = x.shape
    return _mm(x.reshape(b * t, k), w).reshape(b, t, w.shape[1])


VMEM_LIMIT = 56 * 1024 * 1024
BF16 = jnp.bfloat16


def _params(*sem):
    return pltpu.CompilerParams(dimension_semantics=sem, vmem_limit_bytes=VMEM_LIMIT)


def _mm_ws(x, w, l, tn, tm=1024):
    m, k = x.shape
    n = w.shape[2]
    tm = min(tm, m)
    cast = w.dtype != BF16

    def kern(x_ref, w_ref, o_ref, *scr):
        if cast:
            @pl.when(pl.program_id(1) == 0)
            def _():
                scr[0][...] = w_ref[...].astype(BF16)
            wv = scr[0][...]
        else:
            wv = w_ref[...]
        o_ref[...] = jnp.dot(x_ref[...], wv, preferred_element_type=jnp.float32)

    return pl.pallas_call(
        kern,
        grid=(n // tn, m // tm),
        in_specs=[pl.BlockSpec((tm, k), lambda j, i: (i, 0)),
                  pl.BlockSpec((None, k, tn), lambda j, i: (l, 0, j))],
        out_specs=pl.BlockSpec((tm, tn), lambda j, i: (i, j)),
        out_shape=jax.ShapeDtypeStruct((m, n), jnp.float32),
        scratch_shapes=[pltpu.VMEM((k, tn), BF16)] if cast else [],
        compiler_params=_params("arbitrary", "arbitrary"),
        name="mm_ws",
    )(x, w)


def _branch_merge(oa, ob, oc, wb, l, mg, tn=512, tm=1024):
    m, kb = oa.shape
    n = wb.shape[3]
    tm = min(tm, m)
    nj = n // tn

    def kern(a_ref, b_ref, c_ref, wa_ref, wb_ref, wc_ref, ga_ref, gb_ref, gc_ref, o_ref, sa, sb, sc):
        @pl.when(pl.program_id(1) == 0)
        def _():
            sa[...] = wa_ref[...].astype(BF16)
            sb[...] = wb_ref[...].astype(BF16)
            sc[...] = wc_ref[...].astype(BF16)

        def term(x_ref, s_ref, g_ref):
            return jax.nn.sigmoid(g_ref[...]) * jnp.dot(x_ref[...], s_ref[...], preferred_element_type=jnp.float32)

        o_ref[...] = (term(a_ref, sa, ga_ref) + term(b_ref, sb, gb_ref) + term(c_ref, sc, gc_ref)).astype(BF16)

    xspec = pl.BlockSpec((tm, kb), lambda j, i: (i, 0))

    def wspec(br):
        return pl.BlockSpec((None, None, kb, tn), lambda j, i: (l, br, 0, j))

    def gspec(br):
        return pl.BlockSpec((tm, tn), lambda j, i: (i, br * nj + j))

    return pl.pallas_call(
        kern,
        grid=(nj, m // tm),
        in_specs=[xspec, xspec, xspec, wspec(0), wspec(1), wspec(2), gspec(0), gspec(1), gspec(2)],
        out_specs=pl.BlockSpec((tm, tn), lambda j, i: (i, j)),
        out_shape=jax.ShapeDtypeStruct((m, n), BF16),
        scratch_shapes=[pltpu.VMEM((kb, tn), BF16)] * 3,
        compiler_params=_params("arbitrary", "arbitrary"),
        name="branch_merge",
    )(oa, ob, oc, wb, wb, wb, mg, mg, mg)


def _mm_norm_res(x, w, l, resid, g_post, g_next, tm=512, tk=1024):
    m, k = x.shape
    n = w.shape[2]
    tm = min(tm, m)
    nk = k // tk

    def kern(x_ref, w_ref, r_ref, gp_ref, gn_ref, xo_ref, ho_ref, acc):
        kk = pl.program_id(1)

        @pl.when(kk == 0)
        def _():
            acc[...] = jnp.zeros_like(acc)

        acc[...] += jnp.dot(x_ref[...], w_ref[...], preferred_element_type=jnp.float32)

        @pl.when(kk == nk - 1)
        def _():
            z = acc[...]
            y = z * lax.rsqrt(jnp.mean(z * z, axis=-1, keepdims=True) + NORM_EPS) * gp_ref[...]
            xn = r_ref[...] + y
            xo_ref[...] = xn
            hn = xn * lax.rsqrt(jnp.mean(xn * xn, axis=-1, keepdims=True) + NORM_EPS) * gn_ref[...]
            ho_ref[...] = hn.astype(BF16)

    row = pl.BlockSpec((tm, n), lambda i, kk: (i, 0))
    gain = pl.BlockSpec((1, n), lambda i, kk: (0, 0))
    return pl.pallas_call(
        kern,
        grid=(m // tm, nk),
        in_specs=[pl.BlockSpec((tm, tk), lambda i, kk: (i, kk)),
                  pl.BlockSpec((None, tk, n), lambda i, kk: (l, kk, 0)),
                  row, gain, gain],
        out_specs=[row, row],
        out_shape=[jax.ShapeDtypeStruct((m, n), jnp.float32), jax.ShapeDtypeStruct((m, n), BF16)],
        scratch_shapes=[pltpu.VMEM((tm, n), jnp.float32)],
        compiler_params=_params("arbitrary", "arbitrary"),
        name="mm_norm_res",
    )(x, w, resid, g_post, g_next)


def _ffn_up_act(h, w_up, l, conv_w, conv_b, conv0, t_len, tm=512, tn=512):
    m, k = h.shape
    f = conv_w.shape[2]
    bsz = m // t_len
    nj = f // tn
    tpb = t_len // tm
    taps = FFN_CONV - 1

    def kern(h_ref, wg_ref, wv_ref, cw_ref, cb_ref, c0_ref, act_ref, st_ref, sg, sv, pg, pv):
        i = pl.program_id(1)

        @pl.when(i == 0)
        def _():
            sg[...] = wg_ref[...].astype(BF16)
            sv[...] = wv_ref[...].astype(BF16)

        @pl.when(i % tpb == 0)
        def _():
            pg[...] = c0_ref[:, 0, :]
            pv[...] = c0_ref[:, 1, :]

        rid = lax.broadcasted_iota(jnp.int32, (tm, tn), 0)

        def conv(u, prev, half):
            u1 = jnp.where(rid == 0, prev[1:2], pltpu.roll(u, 1, axis=0))
            u2 = jnp.where(rid == 0, prev[0:1], jnp.where(rid == 1, prev[1:2], pltpu.roll(u, 2, axis=0)))
            return (cb_ref[half:half + 1, :] + u2 * cw_ref[0, half:half + 1, :]
                    + u1 * cw_ref[1, half:half + 1, :] + u * cw_ref[2, half:half + 1, :])

        ug = jnp.dot(h_ref[...], sg[...], preferred_element_type=jnp.float32)
        uv = jnp.dot(h_ref[...], sv[...], preferred_element_type=jnp.float32)
        act_ref[...] = (jax.nn.gelu(conv(ug, pg[...], 0)) * conv(uv, pv[...], 1)).astype(BF16)
        pg[...] = ug[tm - taps:]
        pv[...] = uv[tm - taps:]

        @pl.when(i % tpb == tpb - 1)
        def _():
            st_ref[:, 0, :] = ug[tm - taps:]
            st_ref[:, 1, :] = uv[tm - taps:]

    def wspec(off):
        return pl.BlockSpec((None, k, tn), lambda j, i: (l, 0, j + off))

    stspec = pl.BlockSpec((None, taps, 2, tn), lambda j, i: (i // tpb, 0, 0, j))
    return pl.pallas_call(
        kern,
        grid=(nj, m // tm),
        in_specs=[pl.BlockSpec((tm, k), lambda j, i: (i, 0)), wspec(0), wspec(nj),
                  pl.BlockSpec((FFN_CONV, 2, tn), lambda j, i: (0, 0, j)),
                  pl.BlockSpec((2, tn), lambda j, i: (0, j)), stspec],
        out_specs=[pl.BlockSpec((tm, tn), lambda j, i: (i, j)), stspec],
        out_shape=[jax.ShapeDtypeStruct((m, f), BF16), jax.ShapeDtypeStruct((bsz, taps, 2, f), jnp.float32)],
        scratch_shapes=[pltpu.VMEM((k, tn), BF16), pltpu.VMEM((k, tn), BF16),
                        pltpu.VMEM((taps, tn), jnp.float32), pltpu.VMEM((taps, tn), jnp.float32)],
        compiler_params=_params("arbitrary", "arbitrary"),
        name="ffn_up_act",
    )(h, w_up, w_up, conv_w, conv_b, conv0)


PAGES_PER_STEP = 8


def _past_compress(pool, l, page_table, w2, transposed=False):
    bsz, n_pages = page_table.shape
    cpp = PAGE_SIZE // CMP_STRIDE
    n_steps = n_pages // PAGES_PER_STEP
    width = 2 * NSA_KV_HEADS * HEAD_DIM
    pairs = width // (2 * HEAD_DIM)
    rows = PAGES_PER_STEP * cpp

    def kern(pt_ref, *refs):
        pages = refs[:PAGES_PER_STEP]
        w_ref, o_ref = refs[PAGES_PER_STEP], refs[PAGES_PER_STEP + 1]
        tiles = [p[...].T if transposed else p[...] for p in pages]
        x = jnp.concatenate(tiles, axis=0).reshape(rows, CMP_STRIDE, width)
        acc = [jnp.zeros((rows, 4 * HEAD_DIM), jnp.float32) for _ in range(pairs)]
        for j in range(CMP_STRIDE):
            xj = x[:, j, :].astype(BF16)
            for pr in range(pairs):
                kind = pr // (pairs // 2)
                acc[pr] = acc[pr] + jnp.dot(xj[:, pr * 2 * HEAD_DIM:(pr + 1) * 2 * HEAD_DIM], w_ref[kind, j],
                                            preferred_element_type=jnp.float32)
        for pr in range(pairs):
            o_ref[:, pr * 4 * HEAD_DIM:(pr + 1) * 4 * HEAD_DIM] = acc[pr]

    def page_spec(jj):
        shape = (None, None, width, PAGE_SIZE) if transposed else (None, None, PAGE_SIZE, width)
        return pl.BlockSpec(shape, lambda b, s, pt: (l, pt[b, s * PAGES_PER_STEP + jj], 0, 0))

    return pl.pallas_call(
        kern,
        grid_spec=pltpu.PrefetchScalarGridSpec(
            num_scalar_prefetch=1,
            grid=(bsz, n_steps),
            in_specs=[page_spec(jj) for jj in range(PAGES_PER_STEP)]
            + [pl.BlockSpec(w2.shape, lambda b, s, pt: (0, 0, 0, 0))],
            out_specs=pl.BlockSpec((None, rows, 2 * width), lambda b, s, pt: (b, s, 0)),
        ),
        out_shape=jax.ShapeDtypeStruct((bsz, n_pages * cpp, 2 * width), jnp.float32),
        compiler_params=_params("arbitrary", "arbitrary"),
        name="past_compress",
    )(page_table, *([pool] * PAGES_PER_STEP), w2)


def _rmsnorm(x, g):
    xf = x.astype(jnp.float32)
    y = xf * lax.rsqrt(jnp.mean(xf * xf, axis=-1, keepdims=True) + NORM_EPS)
    return (y * g.astype(jnp.float32)).astype(x.dtype)


def _causal_conv(x, buf, w, b):
    k, t = w.shape[0], x.shape[1]
    xp = jnp.concatenate([buf.astype(x.dtype), x], axis=1)
    y = b + xp[:, 0:t] * w[0]
    for j in range(1, k):
        y = y + xp[:, j:j + t] * w[j]
    return y, xp[:, xp.shape[1] - (k - 1):]


def _rope(x, pos):
    half = x.shape[-1] // 2
    inv = ROPE_THETA ** (-jnp.arange(half, dtype=jnp.float32) / half)
    ang = pos.astype(jnp.float32)[:, None] * inv[None, :]
    cos = jnp.cos(ang)[None, :, None, :].astype(x.dtype)
    sin = jnp.sin(ang)[None, :, None, :].astype(x.dtype)
    x1, x2 = x[..., :half], x[..., half:]
    return jnp.concatenate([x1 * cos - x2 * sin, x2 * cos + x1 * sin], axis=-1)


def _masked_softmax(s, mask):
    p = jax.nn.softmax(jnp.where(mask, s, NEG_INF), axis=-1)
    return jnp.where(mask, p, 0.0)


def _lin_combine(left, right):
    a1, b1 = left
    a2, b2 = right
    return a1 * a2, a2 * b1 + b2


def _pad_rows(z, mult):
    pad = -z.shape[1] % mult
    return jnp.pad(z, ((0, 0), (0, pad)) + ((0, 0),) * (z.ndim - 2))


def _rg_lru(xa, h0, conv_buf, conv_w, conv_b, gate_w, gate_b, lam):
    f32 = jnp.float32
    bsz, t, _ = xa.shape
    xc, new_buf = _causal_conv(xa, conv_buf, conv_w, conv_b)
    xb = xc.reshape(bsz, t, LRU_BLOCKS, LRU_BLOCK_DIM)
    gates = jnp.einsum('btnd,gnde->gbtne', xb, gate_w).reshape(2, bsz, t, LRU_WIDTH)
    gates = gates.astype(f32) + gate_b.astype(f32)[:, None, None, :]
    r, i = jax.nn.sigmoid(gates[0]), jax.nn.sigmoid(gates[1])
    log_a = -LRU_C * r * jax.nn.softplus(-lam.astype(f32))
    a = jnp.exp(log_a)
    b = jnp.sqrt(-jnp.expm1(2.0 * log_a)) * (i * xc.astype(f32))
    b = b.at[:, 0].add(a[:, 0] * h0.astype(f32))
    _, h = lax.associative_scan(_lin_combine, (a, b), axis=1)
    return h.astype(xa.dtype), h[:, -1].astype(xa.dtype), new_buf


def _lru_glue(xa, h0, conv_buf, conv_w, conv_b, gate_w, gate_b, lam, bsz, t):
    o, h_last, new_buf = _rg_lru(xa.reshape(bsz, t, LRU_WIDTH), h0, conv_buf, conv_w, conv_b, gate_w, gate_b, lam)
    return o.reshape(bsz * t, LRU_WIDTH), h_last, new_buf


SUBLANES = 8


def _lru_kernel(x_ref, cw_ref, cb_ref, gw_ref, gb_ref, sp_ref, h0_ref, c0_ref, o_ref, hl_ref,
                a_s, b_s, h_s, hc, prev, *, tpb):
    i = pl.program_id(0)
    tm, width = x_ref.shape
    taps = LRU_CONV - 1

    @pl.when(i % tpb == 0)
    def _():
        hc[...] = h0_ref[...]
        prev[...] = c0_ref[...]

    x = x_ref[...]
    rid = lax.broadcasted_iota(jnp.int32, (tm, width), 0)
    xc = cb_ref[...]
    for j in range(taps):
        back = taps - j
        u = pltpu.roll(x, back, axis=0)
        for r in range(back):
            u = jnp.where(rid == r, prev[taps - back + r:taps - back + r + 1, :], u)
        xc = xc + u * cw_ref[j:j + 1, :]
    xc = xc + x * cw_ref[taps:taps + 1, :]
    prev[...] = x[tm - taps:]

    gates = jnp.dot(xc.astype(BF16), gw_ref[...], preferred_element_type=jnp.float32) + gb_ref[...]
    r_g = jax.nn.sigmoid(gates[:, :width])
    i_g = jax.nn.sigmoid(gates[:, width:])
    log_a = -LRU_C * r_g * sp_ref[...]
    a = jnp.exp(log_a)
    a_s[...] = a
    b_s[...] = jnp.sqrt(-jnp.tanh(log_a) * (jnp.exp(2.0 * log_a) + 1.0)) * (i_g * xc)

    def group(gidx, h):
        base = pl.multiple_of(gidx * SUBLANES, SUBLANES)
        a8 = a_s[pl.ds(base, SUBLANES), :]
        b8 = b_s[pl.ds(base, SUBLANES), :]
        rows = []
        for r in range(SUBLANES):
            h = a8[r:r + 1] * h + b8[r:r + 1]
            rows.append(h)
        h_s[pl.ds(base, SUBLANES), :] = jnp.concatenate(rows, axis=0)
        return h

    h_last = lax.fori_loop(0, tm // SUBLANES, group, hc[...])
    hc[...] = h_last
    hl_ref[...] = h_last
    o_ref[...] = h_s[...].astype(o_ref.dtype)


def _lru_prompt(xa, h0, conv_buf, conv_w, conv_b, gate_w, gate_b, lam, bsz, t, tm=512):
    m, width = xa.shape
    tpb = t // tm
    taps = LRU_CONV - 1
    eye = jnp.eye(LRU_BLOCKS, dtype=gate_w.dtype)
    gw = jnp.einsum('gnde,nm->gndme', gate_w, eye).reshape(2, width, width)
    gw = jnp.concatenate([gw[0], gw[1]], axis=1).astype(BF16)
    gb = gate_b.reshape(1, 2 * width)
    sp = jax.nn.softplus(-lam.astype(jnp.float32)).reshape(1, width)
    row = lambda n: pl.BlockSpec((n, width), lambda i: (0, 0))
    o, h_last = pl.pallas_call(
        functools.partial(_lru_kernel, tpb=tpb),
        grid=(m // tm,),
        in_specs=[pl.BlockSpec((tm, width), lambda i: (i, 0)), row(LRU_CONV), row(1),
                  pl.BlockSpec((width, 2 * width), lambda i: (0, 0)),
                  pl.BlockSpec((1, 2 * width), lambda i: (0, 0)), row(1),
                  pl.BlockSpec((None, 1, width), lambda i: (i // tpb, 0, 0)),
                  pl.BlockSpec((None, taps, width), lambda i: (i // tpb, 0, 0))],
        out_specs=[pl.BlockSpec((tm, width), lambda i: (i, 0)),
                   pl.BlockSpec((None, 1, width), lambda i: (i // tpb, 0, 0))],
        out_shape=[jax.ShapeDtypeStruct((m, width), BF16), jax.ShapeDtypeStruct((bsz, 1, width), jnp.float32)],
        scratch_shapes=[pltpu.VMEM((tm, width), jnp.float32)] * 3
        + [pltpu.VMEM((1, width), jnp.float32), pltpu.VMEM((taps, width), jnp.float32)],
        compiler_params=_params("arbitrary"),
        name="rg_lru",
    )(xa, conv_w, conv_b.reshape(1, width), gw, gb, sp, h0.reshape(bsz, 1, width), conv_buf)
    new_buf = xa.reshape(bsz, t, width)[:, t - taps:]
    return o, h_last.reshape(bsz, width), new_buf


def _compress(z, w, b):
    bsz, length, g, d = z.shape
    ch = z.reshape(bsz, length // CMP_STRIDE, CMP_STRIDE, g, d)
    head = jnp.einsum('bcjgd,jde->bcge', ch, w[:CMP_STRIDE])
    tail = jnp.einsum('bcjgd,jde->bcge', ch, w[CMP_STRIDE:])
    return head[:, :-1] + tail[:, 1:] + b


def _nsa_core(q, q_rot, qpos, kc, vc, selected, kw, vw, kwpos, gates):
    f32 = jnp.float32
    bsz, t = q.shape[0], q.shape[1]
    hpg = NSA_HEADS // NSA_KV_HEADS
    scale = HEAD_DIM ** -0.5
    qg = q.reshape(bsz, t, NSA_KV_HEADS, hpg, HEAD_DIM)
    qrg = q_rot.reshape(bsz, t, NSA_KV_HEADS, hpg, HEAD_DIM)
    n_cmp = kc.shape[1]
    cmp_end = jnp.arange(n_cmp) * CMP_STRIDE + (CMP_BLOCK - 1)
    m_c = (cmp_end[None, :] <= qpos[:, None])[None, :, None, None, :]
    p_c = _masked_softmax(jnp.einsum('btghd,bngd->btghn', qg, kc).astype(f32) * scale, m_c)
    o_c = jnp.einsum('btghn,bngd->btghd', p_c.astype(vc.dtype), vc)
    per = SLC_BLOCK // CMP_STRIDE
    n_slc = (n_cmp + 1) // per
    imp = jnp.pad(p_c.sum(axis=3), ((0, 0), (0, 0), (0, 0), (0, 1))).reshape(bsz, t, NSA_KV_HEADS, n_slc, per)
    imp = imp.sum(-1) + jnp.pad(imp[..., :-1, per - 1], ((0, 0), (0, 0), (0, 0), (1, 0)))
    blk = jnp.arange(n_slc)[None, :]
    qblk = (qpos // SLC_BLOCK)[:, None]
    valid = blk * SLC_BLOCK <= qpos[:, None]
    forced = (blk == 0) | (blk == qblk) | (blk == qblk - 1)
    score = jnp.where(valid[None, :, None, :], imp + FORCE_BONUS * forced[None, :, None, :], NEG_INF)
    n_top = min(N_SELECT, n_slc)
    _, idx = lax.top_k(score, n_top)
    o_s = selected(qrg, idx)
    dpos = qpos[:, None] - kwpos[None, :]
    m_w = ((dpos >= 0) & (dpos < WINDOW) & (kwpos[None, :] >= 0))[None, :, None, None, :]
    p_w = _masked_softmax(jnp.einsum('btghd,bkgd->btghk', qrg, kw).astype(f32) * scale, m_w)
    o_w = jnp.einsum('btghk,bkgd->btghd', p_w.astype(vw.dtype), vw)
    gt = jax.nn.sigmoid(gates.astype(f32)).astype(q.dtype).reshape(bsz, t, NSA_KV_HEADS, hpg, 3, 1)
    o = gt[..., 0, :] * o_c + gt[..., 1, :] * o_s + gt[..., 2, :] * o_w
    return o.reshape(bsz, t, NSA_WIDTH)


HPG = NSA_HEADS // NSA_KV_HEADS
CMP_PER_SLC = SLC_BLOCK // CMP_STRIDE
SEL_CHUNK = 512
WIN_KEYS = WINDOW + Q_BLOCK
DROPPED = -3e38


def _softmax_rows(s, ok):
    sm = jnp.where(ok[None], s, NEG_INF)
    e = jnp.exp(sm - jnp.max(sm, axis=-1, keepdims=True))
    p = e / jnp.sum(e, axis=-1, keepdims=True)
    return jnp.where(ok[None], p, 0.0)


def _nsa_prompt_kernel(q_ref, qr_ref, g_ref, kct_ref, vc_ref, kst_ref, vs_ref, kwt_ref, vw_ref,
                       impt_ref, exp_ref, o_ref, *, n_cmp):
    f32, bf = jnp.float32, jnp.bfloat16
    i = pl.program_id(2)
    start = i * Q_BLOCK
    rows = HPG * Q_BLOCK
    n_cpad = kct_ref.shape[-1]
    n_slc = impt_ref.shape[0]
    scale = HEAD_DIM ** -0.5
    q = (q_ref[0, 0, 0] * scale).astype(bf)
    qr = (qr_ref[0, 0, 0] * scale).astype(bf)
    t_col = start + lax.broadcasted_iota(jnp.int32, (Q_BLOCK, 1), 0)

    s = jnp.dot(q, kct_ref[0, 0], preferred_element_type=f32).reshape(HPG, Q_BLOCK, n_cpad)
    n_io = lax.broadcasted_iota(jnp.int32, (Q_BLOCK, n_cpad), 1)
    ok_c = jnp.where(n_io < n_cmp, n_io * CMP_STRIDE + (CMP_BLOCK - 1), 2 ** 30) <= t_col
    p = _softmax_rows(s, ok_c)
    o_c = jnp.dot(p.reshape(rows, n_cpad).astype(bf), vc_ref[0, 0], preferred_element_type=f32)

    psum = p[0] + p[1] + p[2] + p[3]
    hi = psum.astype(bf)
    r1 = psum - hi.astype(f32)
    mid = r1.astype(bf)
    lo = (r1 - mid.astype(f32)).astype(bf)
    nt = (((1,), (1,)), ((), ()))
    imp_t = (lax.dot_general(impt_ref[...], hi, nt, preferred_element_type=f32)
             + lax.dot_general(impt_ref[...], mid, nt, preferred_element_type=f32)
             + lax.dot_general(impt_ref[...], lo, nt, preferred_element_type=f32))
    j_io = lax.broadcasted_iota(jnp.int32, (n_slc, Q_BLOCK), 0)
    t_row = start + lax.broadcasted_iota(jnp.int32, (n_slc, Q_BLOCK), 1)
    qblk = t_row // SLC_BLOCK
    forced = jnp.where(j_io == 0, 1.0, 0.0) + jnp.where(j_io == qblk, 1.0, 0.0) + jnp.where(j_io == qblk - 1, 1.0, 0.0)
    forced = jnp.minimum(forced, 1.0)
    score = jnp.where(j_io * SLC_BLOCK <= t_row, imp_t + FORCE_BONUS * forced, NEG_INF)

    def pick(_, carry):
        sc, sel = carry
        best = jnp.max(sc, axis=0, keepdims=True)
        first = jnp.min(jnp.where(sc == best, j_io, n_slc), axis=0, keepdims=True)
        hit = j_io == first
        return jnp.where(hit, DROPPED, sc), jnp.where(hit, 1.0, sel)

    _, sel_t = lax.fori_loop(0, min(N_SELECT, n_slc), pick, (score, jnp.zeros((n_slc, Q_BLOCK), f32)))
    sel = sel_t.T.astype(bf)

    def chunk(c, carry):
        m, l, acc = carry
        off = pl.multiple_of(c * SEL_CHUNK, SEL_CHUNK)
        kt = kst_ref[0, 0, :, pl.ds(off, SEL_CHUNK)]
        v = vs_ref[0, 0, pl.ds(off, SEL_CHUNK), :]
        sc = jnp.dot(qr, kt, preferred_element_type=f32).reshape(HPG, Q_BLOCK, SEL_CHUNK)
        chosen = jnp.dot(sel, exp_ref[:, pl.ds(off, SEL_CHUNK)], preferred_element_type=f32)
        kpos = off + lax.broadcasted_iota(jnp.int32, (Q_BLOCK, SEL_CHUNK), 1)
        ok = jnp.where(kpos <= t_col, chosen, 0.0) > 0.5
        sc = jnp.where(ok[None], sc, NEG_INF)
        m_new = jnp.maximum(m, jnp.max(sc, axis=-1, keepdims=True))
        alpha = jnp.exp(m - m_new)
        pe = jnp.exp(sc - m_new)
        l = alpha * l + jnp.sum(pe, axis=-1, keepdims=True)
        pv = jnp.dot(pe.reshape(rows, SEL_CHUNK).astype(bf), v, preferred_element_type=f32)
        return m_new, l, alpha * acc + pv.reshape(HPG, Q_BLOCK, HEAD_DIM)

    n_chunks = (start + Q_BLOCK + SEL_CHUNK - 1) // SEL_CHUNK
    init = (jnp.full((HPG, Q_BLOCK, 1), NEG_INF, f32), jnp.zeros((HPG, Q_BLOCK, 1), f32),
            jnp.zeros((HPG, Q_BLOCK, HEAD_DIM), f32))
    _, l_s, acc_s = lax.fori_loop(0, n_chunks, chunk, init)
    o_s = (acc_s / l_s).reshape(rows, HEAD_DIM)

    n_keys = kwt_ref.shape[-1]
    wk = min(WIN_KEYS, n_keys)
    k0 = pl.multiple_of(jnp.maximum(start + Q_BLOCK - wk, 0), Q_BLOCK)
    kt = kwt_ref[0, 0, :, pl.ds(k0, wk)]
    v = vw_ref[0, 0, pl.ds(k0, wk), :]
    sw = jnp.dot(qr, kt, preferred_element_type=f32).reshape(HPG, Q_BLOCK, wk)
    dpos = t_col - (k0 + lax.broadcasted_iota(jnp.int32, (Q_BLOCK, wk), 1))
    ok_w = jnp.where(dpos >= 0, dpos, WINDOW) < WINDOW
    pw = _softmax_rows(sw, ok_w)
    o_w = jnp.dot(pw.reshape(rows, wk).astype(bf), v, preferred_element_type=f32)

    gt = jax.nn.sigmoid(g_ref[0, 0])
    heads = []
    for h in range(HPG):
        r = slice(h * Q_BLOCK, (h + 1) * Q_BLOCK)
        heads.append(gt[:, 3 * h:3 * h + 1] * o_c[r] + gt[:, 3 * h + 1:3 * h + 2] * o_s[r]
                     + gt[:, 3 * h + 2:3 * h + 3] * o_w[r])
    o_ref[0] = jnp.concatenate(heads, axis=1).astype(o_ref.dtype)


def _nsa_prep_kernel(q_ref, kv_ref, cos_ref, sin_ref, qb_ref, qrb_ref, kvr_ref, kst_ref, kwt_ref, vs_ref, vw_ref):
    cos, sin = cos_ref[...], sin_ref[...]
    lane = lax.broadcasted_iota(jnp.int32, (Q_BLOCK, LANES), 1)
    first_half = (lane % HEAD_DIM) < HEAD_DIM // 2

    def rope(x):
        partner = jnp.where(first_half, pltpu.roll(x, LANES - HEAD_DIM // 2, axis=1),
                            pltpu.roll(x, HEAD_DIM // 2, axis=1))
        return x * cos + partner * sin

    for c in range(NSA_WIDTH // LANES):
        x = q_ref[:, c * LANES:(c + 1) * LANES]
        xr = rope(x)
        for e in range(LANES // HEAD_DIM):
            g, h = divmod(c * (LANES // HEAD_DIM) + e, HPG)
            rows = slice(h * Q_BLOCK, (h + 1) * Q_BLOCK)
            qb_ref[g, rows, :] = x[:, e * HEAD_DIM:(e + 1) * HEAD_DIM].astype(BF16)
            qrb_ref[g, rows, :] = xr[:, e * HEAD_DIM:(e + 1) * HEAD_DIM].astype(BF16)

    kind_w = NSA_KV_HEADS * HEAD_DIM
    for kind in range(6):
        for c in range(kind_w // LANES):
            col = kind * kind_w + c * LANES
            x = kv_ref[:, col:col + LANES]
            if kind in (2, 4):
                x = rope(x)
                t_ref = kst_ref if kind == 2 else kwt_ref
                t_ref[c * LANES:(c + 1) * LANES, :] = x.T.astype(BF16)
            if kind in (3, 5):
                v_ref = vs_ref if kind == 3 else vw_ref
                for e in range(LANES // HEAD_DIM):
                    v_ref[c * (LANES // HEAD_DIM) + e] = x[:, e * HEAD_DIM:(e + 1) * HEAD_DIM].astype(BF16)
            kvr_ref[:, col:col + LANES] = x


def _rope_tables(pos):
    half = HEAD_DIM // 2
    inv = ROPE_THETA ** (-jnp.arange(half, dtype=jnp.float32) / half)
    ang = pos.astype(jnp.float32)[:, None] * inv[None, :]
    cos, sin = jnp.cos(ang), jnp.sin(ang)
    reps = LANES // HEAD_DIM
    return jnp.tile(jnp.concatenate([cos, cos], axis=1), (1, reps)), jnp.tile(jnp.concatenate([-sin, sin], axis=1), (1, reps))


def _nsa_prompt(q, kv, gates, phi, phi_b, bsz, s):
    g, d = NSA_KV_HEADS, HEAD_DIM
    n_qb = s // Q_BLOCK
    rows = HPG * Q_BLOCK
    kind_w = g * d
    cos, sin = _rope_tables(jnp.arange(s))
    tile = lambda w: pl.BlockSpec((Q_BLOCK, w), lambda b, i: (b * n_qb + i, 0))
    tab = pl.BlockSpec((Q_BLOCK, LANES), lambda b, i: (i, 0))
    qb_spec = pl.BlockSpec((None, g, None, rows, d), lambda b, i: (b, 0, i, 0, 0))
    kt_spec = pl.BlockSpec((None, kind_w, Q_BLOCK), lambda b, i: (b, 0, i))
    v_spec = pl.BlockSpec((None, g, Q_BLOCK, d), lambda b, i: (b, 0, i, 0))
    qb_shape = jax.ShapeDtypeStruct((bsz, g, n_qb, rows, d), BF16)
    kt_shape = jax.ShapeDtypeStruct((bsz, kind_w, s), BF16)
    v_shape = jax.ShapeDtypeStruct((bsz, g, s, d), BF16)
    qb, qrb, kv_rot, kst, kwt, vs, vw = pl.pallas_call(
        _nsa_prep_kernel,
        grid=(bsz, n_qb),
        in_specs=[tile(NSA_WIDTH), tile(N_KV_COLS), tab, tab],
        out_specs=[qb_spec, qb_spec, tile(N_KV_COLS), kt_spec, kt_spec, v_spec, v_spec],
        out_shape=[qb_shape, qb_shape, jax.ShapeDtypeStruct(kv.shape, jnp.float32), kt_shape, kt_shape, v_shape, v_shape],
        compiler_params=_params("arbitrary", "arbitrary"),
        name="nsa_prep",
    )(q, kv, cos, sin)

    pages = jnp.arange(bsz * n_qb, dtype=jnp.int32).reshape(bsz, n_qb)
    ht = _past_compress(kv.reshape(1, bsz * n_qb, Q_BLOCK, N_KV_COLS), 0, pages, _pair_weights(phi))
    ht = ht.reshape(bsz, ht.shape[1], 2, g, 2, d)
    n_cmp = ht.shape[1] - 1
    n_cpad = -(-(n_cmp + 1) // LANES) * LANES
    n_slc = (n_cmp + 1) // CMP_PER_SLC
    pad_c = ((0, 0), (0, n_cpad - n_cmp), (0, 0), (0, 0))
    kc = jnp.pad(ht[:, :-1, 0, :, 0] + ht[:, 1:, 0, :, 1] + phi_b[0], pad_c)
    vc = jnp.pad(ht[:, :-1, 1, :, 0] + ht[:, 1:, 1, :, 1] + phi_b[1], pad_c)
    kct = kc.transpose(0, 2, 3, 1).astype(BF16)
    vcg = vc.transpose(0, 2, 1, 3).astype(BF16)
    gates_g = gates[:, :N_NSA_GATES].reshape(bsz, s, g, 3 * HPG).transpose(0, 2, 1, 3)

    n_io = jnp.arange(n_cpad)[None, :]
    j_io = jnp.arange(n_slc)[:, None]
    imp_t = ((n_io >= CMP_PER_SLC * j_io - 1) & (n_io < CMP_PER_SLC * (j_io + 1))).astype(BF16)
    expand = (jnp.arange(s)[None, :] // SLC_BLOCK == j_io).astype(BF16)
    qspec = pl.BlockSpec((1, 1, 1, rows, d), lambda b, gg, i: (b, gg, i, 0, 0))

    def whole(shape):
        return pl.BlockSpec((1, 1) + shape, lambda b, gg, i: (b, gg, 0, 0))

    o = pl.pallas_call(
        functools.partial(_nsa_prompt_kernel, n_cmp=n_cmp),
        grid=(bsz, g, n_qb),
        in_specs=[qspec, qspec,
                  pl.BlockSpec((1, 1, Q_BLOCK, 3 * HPG), lambda b, gg, i: (b, gg, i, 0)),
                  whole((d, n_cpad)), whole((n_cpad, d)),
                  whole((d, s)), whole((s, d)), whole((d, s)), whole((s, d)),
                  pl.BlockSpec((n_slc, n_cpad), lambda b, gg, i: (0, 0)),
                  pl.BlockSpec((n_slc, s), lambda b, gg, i: (0, 0))],
        out_specs=pl.BlockSpec((1, Q_BLOCK, HPG * d), lambda b, gg, i: (b, i, gg)),
        out_shape=jax.ShapeDtypeStruct((bsz, s, NSA_WIDTH), BF16),
        compiler_params=_params("arbitrary", "arbitrary", "arbitrary"),
        name="nsa_prompt",
    )(qb, qrb, gates_g, kct, vcg, kst.reshape(bsz, g, d, s), vs, kwt.reshape(bsz, g, d, s), vw, imp_t, expand)

    n_win = min(WINDOW, s)
    kv_rot = kv_rot.reshape(bsz, s, 6, g, d)
    return o.reshape(bsz * s, NSA_WIDTH), kv_rot[:, :, :4], kv_rot[:, s - n_win:, 4:]


def _pair_weights(phi):
    d = HEAD_DIM
    ht = jnp.concatenate([phi[:, :CMP_STRIDE], phi[:, CMP_STRIDE:]], axis=-1)
    z = jnp.zeros_like(ht)
    return jnp.concatenate([jnp.concatenate([ht, z], axis=-1), jnp.concatenate([z, ht], axis=-1)], axis=-2).astype(BF16)


def _pages_t(cache):
    return cache.transpose(0, 1, 3, 4, 5, 2).reshape(cache.shape[0], cache.shape[1], -1, cache.shape[2])


def _sample_compressed(cache, l, page_table, new_rows, phi, phi_b):
    bsz = page_table.shape[0]
    g, d = NSA_KV_HEADS, HEAD_DIM
    ht = _past_compress(_pages_t(cache), l, page_table, _pair_weights(phi), transposed=True)
    ht = ht.reshape(bsz, ht.shape[1], 2, g, 2, d)
    out = []
    for kind in range(2):
        ch = _pad_rows(new_rows[kind], SLC_BLOCK)
        ch = ch.reshape(bsz, ch.shape[1] // CMP_STRIDE, CMP_STRIDE, g, d)
        head = jnp.concatenate([ht[:, :, kind, :, 0], jnp.einsum('bcjgd,jde->bcge', ch, phi[kind, :CMP_STRIDE])], axis=1)
        tail = jnp.concatenate([ht[:, :, kind, :, 1], jnp.einsum('bcjgd,jde->bcge', ch, phi[kind, CMP_STRIDE:])], axis=1)
        out.append(head[:, :-1] + tail[:, 1:] + phi_b[kind])
    return out


def _nsa_sample(q, kv, gates, phi, phi_b, cache, l, page_table, win_buf):
    bsz, t = q.shape[0], q.shape[1]
    pos = PAST_LEN + jnp.arange(t)
    k_cmp, v_cmp, k_slc, v_slc, k_win, v_win = [kv[:, :, i] for i in range(6)]
    q_rot, k_slc, k_win = _rope(q, pos), _rope(k_slc, pos), _rope(k_win, pos)
    kc, vc = _sample_compressed(cache, l, page_table, (k_cmp, v_cmp), phi, phi_b)
    kw = jnp.concatenate([win_buf[:, :, 0].astype(q.dtype), k_win], axis=1)
    vw = jnp.concatenate([win_buf[:, :, 1].astype(q.dtype), v_win], axis=1)
    n_win = win_buf.shape[1]
    kwpos = PAST_LEN - n_win + jnp.arange(n_win + t)

    def selected(q_rot_g, idx):
        return _sample_selected(q_rot_g, idx, pos, cache, l, page_table, k_slc, v_slc)

    o = _nsa_core(q, q_rot, pos, kc, vc, selected, kw, vw, kwpos, gates)
    rows = jnp.stack([k_cmp, v_cmp, k_slc, v_slc], axis=2)
    win_rows = jnp.stack([k_win, v_win], axis=2)
    return o, rows, win_rows


def _sample_selected_kernel(pt_ref, *refs):
    pages = refs[:PAGES_PER_STEP]
    q_ref, ok_ref, new_ref, oknew_ref, o_ref, m_s, l_s, acc_s = refs[PAGES_PER_STEP:]
    step = pl.program_id(1)
    kw = NSA_KV_HEADS * HEAD_DIM

    @pl.when(step == 0)
    def _():
        m_s[...] = jnp.full_like(m_s, NEG_INF)
        l_s[...] = jnp.zeros_like(l_s)
        acc_s[...] = jnp.zeros_like(acc_s)

    def update(kv_t, ok):
        k_t = kv_t[:kw].astype(BF16)
        v_t = kv_t[kw:].astype(BF16)
        sc = jnp.dot(q_ref[...], k_t, preferred_element_type=jnp.float32)
        sc = jnp.where(ok > 0.5, sc, NEG_INF)
        m_new = jnp.maximum(m_s[...], jnp.max(sc, axis=-1, keepdims=True))
        alpha = jnp.exp(m_s[...] - m_new)
        p = jnp.exp(sc - m_new)
        l_s[...] = alpha * l_s[...] + jnp.sum(p, axis=-1, keepdims=True)
        pv = lax.dot_general(p.astype(BF16), v_t, (((1,), (1,)), ((), ())), preferred_element_type=jnp.float32)
        acc_s[...] = alpha * acc_s[...] + pv
        m_s[...] = m_new

    update(jnp.concatenate([p[...] for p in pages], axis=1), ok_ref[...])

    @pl.when(step == pl.num_programs(1) - 1)
    def _():
        update(new_ref[...], oknew_ref[...])
        o_ref[...] = acc_s[...] / l_s[...]


def _sample_selected(q_rot_g, idx, qpos, cache, l, page_table, k_new, v_new):
    bsz, t = q_rot_g.shape[0], q_rot_g.shape[1]
    g, d = NSA_KV_HEADS, HEAD_DIM
    n_pages = page_table.shape[1]
    past = n_pages * PAGE_SIZE
    rows = g * t * HPG
    kw = g * d
    n_steps = n_pages // PAGES_PER_STEP
    keys_per_step = PAGES_PER_STEP * PAGE_SIZE
    pool = _pages_t(cache)

    qg = (q_rot_g * HEAD_DIM ** -0.5).transpose(0, 2, 1, 3, 4).reshape(bsz, g, t * HPG, d)
    q_bd = (qg[:, :, :, None, :] * jnp.eye(g, dtype=qg.dtype)[None, :, None, :, None]).reshape(bsz, rows, kw).astype(BF16)

    n_blk = past // SLC_BLOCK
    chosen = (idx[..., None] == jnp.arange(n_blk + 1)).any(axis=-2)

    def by_row(z):
        z = jnp.broadcast_to(z.transpose(0, 2, 1, 3)[:, :, :, None, :], (bsz, g, t, HPG, z.shape[-1]))
        return z.reshape(bsz, rows, z.shape[-1]).astype(jnp.float32)

    ok_past = by_row(jnp.repeat(chosen[..., :n_blk], SLC_BLOCK, axis=-1))
    j = jnp.arange(PAGE_SIZE)
    new_ok = (j[None, :] < t) & (past + j[None, :] <= qpos[:, None])
    ok_new = by_row(chosen[..., n_blk:] & new_ok[None, :, None, :])
    new_kv = jnp.concatenate([k_new.reshape(bsz, t, kw), v_new.reshape(bsz, t, kw)], axis=-1)
    new_kv = jnp.pad(new_kv, ((0, 0), (0, PAGE_SIZE - t), (0, 0))).transpose(0, 2, 1)

    def page_spec(jj):
        return pl.BlockSpec((None, None, 2 * kw, PAGE_SIZE),
                            lambda b, s, pt: (l, pt[b, s * PAGES_PER_STEP + jj], 1, 0))

    o = pl.pallas_call(
        _sample_selected_kernel,
        grid_spec=pltpu.PrefetchScalarGridSpec(
            num_scalar_prefetch=1,
            grid=(bsz, n_steps),
            in_specs=[page_spec(jj) for jj in range(PAGES_PER_STEP)]
            + [pl.BlockSpec((None, rows, kw), lambda b, s, pt: (b, 0, 0)),
               pl.BlockSpec((None, rows, keys_per_step), lambda b, s, pt: (b, 0, s)),
               pl.BlockSpec((None, 2 * kw, PAGE_SIZE), lambda b, s, pt: (b, 0, 0)),
               pl.BlockSpec((None, rows, PAGE_SIZE), lambda b, s, pt: (b, 0, 0))],
            out_specs=pl.BlockSpec((None, rows, kw), lambda b, s, pt: (b, 0, 0)),
            scratch_shapes=[pltpu.VMEM((rows, 1), jnp.float32), pltpu.VMEM((rows, 1), jnp.float32),
                            pltpu.VMEM((rows, kw), jnp.float32)],
        ),
        out_shape=jax.ShapeDtypeStruct((bsz, rows, kw), jnp.float32),
        compiler_params=_params("arbitrary", "arbitrary"),
        name="sample_selected",
    )(page_table, *([pool] * PAGES_PER_STEP), q_bd, ok_past, new_kv, ok_new)
    o = o.reshape(bsz, g, t, HPG, g, d)
    o = jnp.stack([o[:, gg, :, :, gg] for gg in range(g)], axis=1)
    return o.transpose(0, 2, 1, 3, 4)


LANES = 128
WKV_T_CHUNK = 64


def _wkv_kernel(w_ref, kk_ref, kka_ref, k_ref, r_ref, v_ref, s0_ref, y_ref, sfin_ref, s_scr):
    c = pl.program_id(0)
    n_vg = s_scr.shape[0]

    @pl.when(c == 0)
    def _():
        s_scr[...] = s0_ref[...]

    def step(t, carry):
        w, kk, kka, k, r = w_ref[t], kk_ref[t], kka_ref[t], k_ref[t], r_ref[t]
        for vg in range(n_vg):
            s = s_scr[vg]
            sa = jnp.sum(s * kk, axis=0, keepdims=True)
            s = s * w - kka * sa + k * v_ref[t, vg:vg + 1, :]
            s_scr[vg] = s
            y_ref[t, vg:vg + 1, :] = jnp.sum(s * r, axis=0, keepdims=True)
        return carry

    lax.fori_loop(0, w_ref.shape[0], step, 0)

    @pl.when(c == pl.num_programs(0) - 1)
    def _():
        sfin_ref[...] = s_scr[...]


def _wkv_scan(r, w, k, v, kk, kka, s0):
    bsz, t, h, n = r.shape
    bh = bsz * h
    vrep = LANES // bh
    n_vg = n // vrep
    tc = min(WKV_T_CHUNK, t)

    def key_tiles(z):
        z = z.transpose(1, 3, 0, 2).reshape(t, n, 1, bh)
        return jnp.broadcast_to(z, (t, n, vrep, bh)).reshape(t, n, LANES)

    v_rows = v.transpose(1, 3, 0, 2).reshape(t, n_vg, LANES)
    s_tiles = s0.transpose(2, 3, 0, 1).reshape(n_vg, vrep, n, bh).transpose(0, 2, 1, 3).reshape(n_vg, n, LANES)
    kspec = pl.BlockSpec((tc, n, LANES), lambda c: (c, 0, 0))
    vspec = pl.BlockSpec((tc, n_vg, LANES), lambda c: (c, 0, 0))
    sspec = pl.BlockSpec((n_vg, n, LANES), lambda c: (0, 0, 0))
    y, s_fin = pl.pallas_call(
        _wkv_kernel,
        grid=(t // tc,),
        in_specs=[kspec] * 5 + [vspec, sspec],
        out_specs=[vspec, sspec],
        out_shape=[jax.ShapeDtypeStruct((t, n_vg, LANES), jnp.float32),
                   jax.ShapeDtypeStruct((n_vg, n, LANES), jnp.float32)],
        scratch_shapes=[pltpu.VMEM((n_vg, n, LANES), jnp.float32)],
        compiler_params=pltpu.CompilerParams(dimension_semantics=("arbitrary",),
                                             vmem_limit_bytes=48 * 1024 * 1024),
        name="wkv_scan",
    )(key_tiles(w), key_tiles(kk), key_tiles(kka), key_tiles(k), key_tiles(r), v_rows, s_tiles)
    y = y.reshape(t, n, bsz, h).transpose(2, 0, 3, 1)
    s_fin = s_fin.reshape(n_vg, n, vrep, bsz, h).transpose(3, 4, 0, 2, 1).reshape(bsz, h, n, n)
    return y, s_fin


def _rwkv7(c, wkv0, shift0, mu, w0, w2, a0, a2, g2, k_k, k_a, r_k, ln_g, ln_b):
    f32 = jnp.float32
    bsz, t, _ = c.shape
    prev = jnp.concatenate([shift0.astype(c.dtype), c[:, :-1]], axis=1)
    cm = c + mu * (prev - c)
    r, k, v, wl, al, gl = jnp.split(cm, RWKV_SPLITS, axis=-1)
    log_w = -jax.nn.softplus(-(w0 + jnp.tanh(wl) @ w2).astype(f32)) - 0.5
    decay = jnp.exp(-jnp.exp(log_w))
    a = jax.nn.sigmoid((a0 + al @ a2).astype(f32))
    g = jax.nn.sigmoid(gl) @ g2

    def heads(z):
        return z.astype(f32).reshape(bsz, t, RWKV_HEADS, RWKV_HEAD_DIM)

    kk = heads(k * k_k)
    kk = kk * lax.rsqrt(jnp.sum(kk * kk, axis=-1, keepdims=True) + 1e-12)
    k = k.astype(f32) * (1.0 + (a - 1.0) * k_a.astype(f32))
    rh, kh, vh, wh, ah = heads(r), heads(k), heads(v), heads(decay), heads(a)

    y, s_fin = _wkv_scan(rh, wh, kh, vh, kk, kk * ah, wkv0.astype(f32))
    mean = jnp.mean(y, axis=-1, keepdims=True)
    var = jnp.mean(jnp.square(y - mean), axis=-1, keepdims=True)
    y = ((y - mean) * lax.rsqrt(var + RWKV_GN_EPS)).reshape(bsz, t, RWKV_WIDTH) * ln_g.astype(f32) + ln_b.astype(f32)
    bonus = (jnp.sum(rh * kh * r_k.astype(f32), axis=-1, keepdims=True) * vh).reshape(bsz, t, RWKV_WIDTH)
    out = ((y + bonus) * g.astype(f32)).astype(c.dtype)
    return out, s_fin.astype(c.dtype), c[:, t - 1:]


def _pad_cols(w, n):
    return jnp.pad(w, ((0, 0), (0, 0), (0, n - w.shape[2])))


def _split_w_in(w_in):
    s = (0,) + IN_SPLITS + (D_IN,)
    seg = [w_in[:, :, s[i]:s[i + 1]] for i in range(6)]
    seg[3] = _pad_cols(seg[3], LANES)
    seg[4] = _pad_cols(seg[4], RW_PAD)
    return [z.astype(BF16) for z in seg]


RW_PAD = 2 * 14 * LANES
IN_TILES = (1024, 1024, N_KV_COLS, LANES, RW_PAD // 2, 1024)


def _layer(x, h, l, P, st, nsa_fn, fuse_ffn):
    lru_h0, lru_conv0, wkv0, shift0, ffn_conv0 = st
    bsz, t, _ = x.shape
    m = bsz * t
    norms = P['norms']
    xa, q, kv, nsa_g, rw, mg = [_mm_ws(h, w, l, tn) for w, tn in zip(P['w_in_seg'], IN_TILES)]
    rw = rw[:, :RWKV_COLS].reshape(bsz, t, RWKV_COLS)
    lru_fn = _lru_prompt if fuse_ffn else _lru_glue
    o_a, lru_h, lru_conv = lru_fn(xa, lru_h0, lru_conv0, P['lru_conv_w'][l], P['lru_conv_b'][l], P['lru_gate_w'][l], P['lru_gate_b'][l], P['lru_lambda'][l], bsz, t)
    o_b, nsa_rows, win_rows = nsa_fn(l, q, kv, nsa_g, bsz, t)
    o_c, wkv, shift = _rwkv7(rw, wkv0, shift0, P['rwkv_mu'][l], P['rwkv_w0'][l], P['rwkv_w2'][l], P['rwkv_a0'][l], P['rwkv_a2'][l], P['rwkv_g2'][l], P['rwkv_k_k'][l], P['rwkv_k_a'][l], P['rwkv_r_k'][l], P['rwkv_ln_g'][l], P['rwkv_ln_b'][l])

    def rows_bf(z):
        return z.reshape(m, BRANCH_WIDTH).astype(BF16)

    merged = _branch_merge(rows_bf(o_a), rows_bf(o_b), rows_bf(o_c), P['w_branch'], l, mg)
    x2, h2 = _mm_norm_res(merged, P['w_out_bf'], l, x.reshape(m, D_MODEL), norms[l, 1][None], norms[l, 2][None])
    if fuse_ffn:
        conv_w = P['ffn_conv_w'][l].reshape(FFN_CONV, 2, D_FF)
        conv_b = P['ffn_conv_b'][l].reshape(2, D_FF)
        act, ffn_conv = _ffn_up_act(h2, P['w_up'], l, conv_w, conv_b,
                                    ffn_conv0.reshape(bsz, FFN_CONV - 1, 2, D_FF), t)
        ffn_conv = ffn_conv.reshape(bsz, FFN_CONV - 1, 2 * D_FF)
    else:
        u = _mm_ws(h2, P['w_up'], l, 1024).reshape(bsz, t, 2 * D_FF)
        u, ffn_conv = _causal_conv(u, ffn_conv0, P['ffn_conv_w'][l], P['ffn_conv_b'][l])
        u_gate, u_val = jnp.split(u, 2, axis=-1)
        act = (jax.nn.gelu(u_gate) * u_val).reshape(m, D_FF).astype(BF16)
    g_next = norms[min(l + 1, DEPTH - 1), 0][None]
    x3, h_next = _mm_norm_res(act, P['w_down_bf'], l, x2, norms[l, 3][None], g_next)
    return x3.reshape(bsz, t, D_MODEL), h_next, (nsa_rows, win_rows, lru_h, lru_conv, wkv, shift, ffn_conv)


def kernel(x_prompt, x_sample, cache_nsa, cache_win, state_lru_h, state_lru_conv, state_rwkv_wkv, state_rwkv_shift, state_ffn_conv, page_table, norms, w_in, lru_conv_w, lru_conv_b, lru_gate_w, lru_gate_b, lru_lambda, nsa_phi, nsa_phi_b, rwkv_mu, rwkv_w0, rwkv_w2, rwkv_a0, rwkv_a2, rwkv_g2, rwkv_k_k, rwkv_k_a, rwkv_r_k, rwkv_ln_g, rwkv_ln_b, w_branch, w_out, w_up, ffn_conv_w, ffn_conv_b, w_down):
    P = {'norms': norms, 'w_in': w_in, 'lru_conv_w': lru_conv_w, 'lru_conv_b': lru_conv_b, 'lru_gate_w': lru_gate_w, 'lru_gate_b': lru_gate_b, 'lru_lambda': lru_lambda, 'rwkv_mu': rwkv_mu, 'rwkv_w0': rwkv_w0, 'rwkv_w2': rwkv_w2, 'rwkv_a0': rwkv_a0, 'rwkv_a2': rwkv_a2, 'rwkv_g2': rwkv_g2, 'rwkv_k_k': rwkv_k_k, 'rwkv_k_a': rwkv_k_a, 'rwkv_r_k': rwkv_r_k, 'rwkv_ln_g': rwkv_ln_g, 'rwkv_ln_b': rwkv_ln_b, 'w_branch': w_branch, 'w_out': w_out, 'w_up': w_up, 'ffn_conv_w': ffn_conv_w, 'ffn_conv_b': ffn_conv_b, 'w_down': w_down}

    def nsa_prompt_fn(l, q, kv, g, bsz, t):
        return _nsa_prompt(q, kv, g, nsa_phi[l], nsa_phi_b[l], bsz, t)

    def nsa_sample_fn(l, q, kv, g, bsz, t):
        q = q.reshape(bsz, t, NSA_HEADS, HEAD_DIM)
        kv = kv.reshape(bsz, t, 6, NSA_KV_HEADS, HEAD_DIM)
        g = g[:, :N_NSA_GATES].reshape(bsz, t, N_NSA_GATES)
        return _nsa_sample(q, kv, g, nsa_phi[l], nsa_phi_b[l], cache_nsa, l, page_table, cache_win[l])

    P['w_in_seg'] = _split_w_in(w_in)
    P['w_out_bf'] = w_out.astype(BF16)
    P['w_down_bf'] = w_down.astype(BF16)

    def first_norm(x):
        return _rmsnorm(x, norms[0, 0]).reshape(-1, D_MODEL).astype(BF16)

    bsz, dt = x_prompt.shape[0], x_prompt.dtype
    zero_state = (jnp.zeros((bsz, LRU_WIDTH), dt), jnp.zeros((bsz, LRU_CONV - 1, LRU_WIDTH), dt), jnp.zeros((bsz, RWKV_HEADS, RWKV_HEAD_DIM, RWKV_HEAD_DIM), dt), jnp.zeros((bsz, 1, RWKV_COLS), dt), jnp.zeros((bsz, FFN_CONV - 1, 2 * D_FF), dt))
    y_p, y_s = x_prompt, x_sample
    h_p, h_s = first_norm(x_prompt), first_norm(x_sample)
    new_p, new_s = [], []
    for l in range(DEPTH):
        y_p, h_p, st_p = _layer(y_p, h_p, l, P, zero_state, nsa_prompt_fn, True)
        y_s, h_s, st_s = _layer(y_s, h_s, l, P, (state_lru_h[l], state_lru_conv[l], state_rwkv_wkv[l], state_rwkv_shift[l], state_ffn_conv[l]), nsa_sample_fn, False)
        new_p.append(st_p)
        new_s.append(st_s)

    def stacked(rows, i):
        return jnp.stack([r[i] for r in rows])

    return (y_p, y_s) + tuple(stacked(new_p, i) for i in range(7)) + tuple(stacked(new_s, i) for i in range(7))
```
